```python
import jax
import jax.numpy as jnp
from jax import lax
import numpy as np

D_MODEL = 4096
BATCH = 8
SEQ = 2048
DEPTH = 2
DEC_BATCH = 16
DEC_SEQ = 16
PAST_LEN = 2048

CHUNK = 64
Q_BLOCK = 128
N_BRANCH = 4
BRANCH_W = D_MODEL // N_BRANCH
MLA_HEADS = 8
MLA_NOPE = 128
MLA_ROPE = 64
MLA_V = BRANCH_W // MLA_HEADS
MLA_Q_RANK = 768
MLA_KV_RANK = 512
ROPE_THETA = 10000.0
RWKV_HD = 64
RWKV_HEADS = BRANCH_W // RWKV_HD
RWKV_W_LORA = 64
RWKV_A_LORA = 64
RWKV_G_LORA = 128
RWKV_GN_EPS = 64e-5
RWKV_IN = 3 * BRANCH_W + RWKV_W_LORA + RWKV_A_LORA + RWKV_G_LORA
FOX_HEADS = 8
FOX_HD = BRANCH_W // FOX_HEADS
MLSTM_HEADS = 8
MLSTM_HD = BRANCH_W // MLSTM_HEADS
MLSTM_CONV = 4
N_EXPERTS = 32
TOP_K = 4
EXPERT_FF = D_MODEL // 2
SWIGLU_LIMIT = 7.0
SWIGLU_ALPHA = 1.702
EPS = 1e-6
NEG_INF = -1e30
MLA_IN = MLA_Q_RANK + MLA_KV_RANK + MLA_ROPE
FOX_IN = 3 * BRANCH_W + FOX_HEADS
MLSTM_IN = 4 * BRANCH_W + 2 * MLSTM_HEADS
GATE_IN = N_BRANCH * D_MODEL
IN_W = MLA_IN + RWKV_IN + FOX_IN + MLSTM_IN + GATE_IN

kernel_name = 'streaming_hybrid_mla_rwkv7_fox_mlstm_moe_step'


def _split(u, sizes):
    out, off = [], 0
    for s in sizes:
        out.append(u[..., off:off + s])
        off += s
    return out


def rmsnorm(x, g):
    xf = x.astype(jnp.float32)
    y = xf * lax.rsqrt(jnp.mean(xf * xf, axis=-1, keepdims=True) + EPS)
    return (y * g.astype(jnp.float32)).astype(x.dtype)


def headnorm(y, eps):
    mu = jnp.mean(y, axis=-1, keepdims=True)
    var = jnp.mean(jnp.square(y - mu), axis=-1, keepdims=True)
    return (y - mu) * lax.rsqrt(var + eps)


def chunk_causal(tq, tk):
    return (tk // CHUNK) <= (tq // CHUNK)


def frame_causal(tq, tk):
    return tk <= tq


def rope(x, pos):
    half = MLA_ROPE // 2
    inv = ROPE_THETA ** (-jnp.arange(half, dtype=jnp.float32) / half)
    ang = pos.astype(jnp.float32)[:, None] * inv[None, :]
    shape = (1, pos.shape[0]) + (1,) * (x.ndim - 3) + (half,)
    cos = jnp.cos(ang).reshape(shape)
    sin = jnp.sin(ang).reshape(shape)
    xf = x.astype(jnp.float32)
    x1, x2 = xf[..., :half], xf[..., half:]
    return jnp.concatenate([x1 * cos - x2 * sin, x2 * cos + x1 * sin], axis=-1).astype(x.dtype)


def blocked_attention(q, k, v, q_pos, k_pos, allowed_fn, fq=None, fk=None):
    B, Tq, H, dq = q.shape
    dv = v.shape[-1]
    blk = Q_BLOCK if Tq % Q_BLOCK == 0 else Tq
    nb = Tq // blk
    scale = dq ** -0.5
    use_bias = fq is not None
    xs = (q.reshape(B, nb, blk, H, dq).swapaxes(0, 1), q_pos.reshape(nb, blk))
    if use_bias:
        xs = xs + (fq.reshape(B, nb, blk, H).swapaxes(0, 1),)
        fk_t = jnp.swapaxes(fk, 1, 2)

    def one_block(args):
        qi, pi = args[0], args[1]
        s = jnp.einsum('bqhd,bkhd->bhqk', qi, k, preferred_element_type=jnp.float32) * scale
        if use_bias:
            s = s + (jnp.swapaxes(args[2], 1, 2)[..., None] - fk_t[:, :, None, :])
        ok = allowed_fn(pi[:, None], k_pos[None, :])
        s = jnp.where(ok[None, None], s, NEG_INF)
        p = jax.nn.softmax(s, axis=-1).astype(v.dtype)
        return jnp.einsum('bhqk,bkhd->bqhd', p, v)

    out = lax.map(one_block, xs)
    return out.swapaxes(0, 1).reshape(B, Tq, H, dv)


def mla(cq, ckv, kr, past_ckv, past_kr, q_norm, kv_norm, w_uq, w_ukv):
    B, T, _ = cq.shape
    P = past_ckv.shape[1]
    q_pos = P + jnp.arange(T, dtype=jnp.int32)
    k_pos = jnp.arange(P + T, dtype=jnp.int32)
    lat_new = rmsnorm(ckv, kv_norm)
    kr_new = rope(kr, q_pos)
    q = (rmsnorm(cq, q_norm) @ w_uq).reshape(B, T, MLA_HEADS, MLA_NOPE + MLA_ROPE)
    q = jnp.concatenate([q[..., :MLA_NOPE], rope(q[..., MLA_NOPE:], q_pos)], axis=-1)
    lat = jnp.concatenate([past_ckv, lat_new], axis=1)
    k_rope = jnp.concatenate([past_kr, kr_new], axis=1)
    kv = (lat @ w_ukv).reshape(B, P + T, MLA_HEADS, MLA_NOPE + MLA_V)
    k = jnp.concatenate([kv[..., :MLA_NOPE],
                         jnp.broadcast_to(k_rope[:, :, None, :], (B, P + T, MLA_HEADS, MLA_ROPE))], axis=-1)
    o = blocked_attention(q, k, kv[..., MLA_NOPE:], q_pos, k_pos, chunk_causal)
    return o.reshape(B, T, BRANCH_W), lat_new, kr_new


def rwkv7(u, prev_u, S0, mu, w0, w_lora, a0, a_lora, g_lora, k_k, k_a, r_k, gn_w, gn_b):
    f32 = jnp.float32
    B, T, _ = u.shape
    u_prev = jnp.concatenate([prev_u[:, None, :], u[:, :-1]], axis=1)
    um = u + (u_prev - u) * mu
    r, k, v, wd, ad, gd = _split(um, [BRANCH_W] * 3 + [RWKV_W_LORA, RWKV_A_LORA, RWKV_G_LORA])
    w_raw = (w0 + jnp.tanh(wd) @ w_lora).astype(f32)
    decay = jnp.exp(-jnp.exp(-jax.nn.softplus(-w_raw) - 0.5))
    a = jax.nn.sigmoid((a0 + ad @ a_lora).astype(f32))
    g = jax.nn.sigmoid(gd) @ g_lora

    def heads(z):
        return z.astype(f32).reshape(B, T, RWKV_HEADS, RWKV_HD)

    kk = heads(k * k_k)
    kk = kk / jnp.maximum(jnp.linalg.norm(kk, axis=-1, keepdims=True), 1e-12)
    k = k.astype(f32) * (1.0 + (a - 1.0) * k_a.astype(f32))
    rh, kh, vh, ah, wh = heads(r), heads(k), heads(v), heads(a), heads(decay)

    def step(S, xs):
        r_t, w_t, k_t, v_t, kk_t, b_t = xs
        sa = jnp.einsum('bhij,bhj->bhi', S, -kk_t)
        S = S * w_t[:, :, None, :] + sa[..., None] * b_t[:, :, None, :] + v_t[..., None] * k_t[:, :, None, :]
        return S, jnp.einsum('bhij,bhj->bhi', S, r_t)

    def tm(z):
        return jnp.swapaxes(z, 0, 1)

    S_fin, ys = lax.scan(step, S0.astype(f32), (tm(rh), tm(wh), tm(kh), tm(vh), tm(kk), tm(kk * ah)))
    y = headnorm(tm(ys), RWKV_GN_EPS).reshape(B, T, BRANCH_W) * gn_w.astype(f32) + gn_b.astype(f32)
    bonus = jnp.sum(rh * kh * r_k.astype(f32), axis=-1, keepdims=True) * vh
    out = (y + bonus.reshape(B, T, BRANCH_W)) * g.astype(f32)
    return out.astype(u.dtype), S_fin.astype(u.dtype), u[:, -1]


def fox(q, k, v, f_raw, past_k, past_v, past_logf, b_f):
    B, T, _ = q.shape
    P = past_k.shape[1]

    def hd(z):
        return z.reshape(B, T, FOX_HEADS, FOX_HD)

    qh, kh, vh = hd(q), hd(k), hd(v)
    logf = jax.nn.log_sigmoid((f_raw + b_f).astype(jnp.float32))
    F = jnp.cumsum(jnp.concatenate([past_logf.astype(jnp.float32), logf], axis=1), axis=1)
    K = jnp.concatenate([past_k, kh], axis=1)
    V = jnp.concatenate([past_v, vh], axis=1)
    o = blocked_attention(qh, K, V, P + jnp.arange(T, dtype=jnp.int32), jnp.arange(P + T, dtype=jnp.int32),
                          frame_causal, F[:, P:], F)
    return o.reshape(B, T, BRANCH_W), kh, vh, logf.astype(q.dtype)


def mlstm_chunkwise(q, k, v, ig, lf, C0, n0, m0):
    f32 = jnp.float32
    B, T, H, d = q.shape
    L = CHUNK if T % CHUNK == 0 else T
    nc = T // L

    def chunks(z):
        return z.astype(f32).reshape((B, nc, L) + z.shape[2:]).swapaxes(0, 1)

    tri = jnp.tril(jnp.ones((L, L), dtype=bool))[None, :, :, None]

    def step(carry, xs):
        C, n, m = carry
        qc, kc, vc, igc, lfc = xs
        b = jnp.cumsum(lfc, axis=1)
        dlog = b[:, :, None, :] - b[:, None, :, :] + igc[:, None, :, :]
        dlog = jnp.where(tri, dlog, -jnp.inf)
        inter = b + m[:, None, :]
        m_t = jnp.maximum(inter, jnp.max(dlog, axis=2))
        dw = jnp.exp(dlog - m_t[:, :, None, :])
        iw = jnp.exp(inter - m_t)
        sw = jnp.einsum('bthd,bshd->btsh', qc, kc) * dw
        num = jnp.einsum('btsh,bshd->bthd', sw, vc) + iw[..., None] * jnp.einsum('bhvk,bthk->bthv', C, qc)
        den = jnp.sum(sw, axis=2) + iw * jnp.einsum('bhk,bthk->bth', n, qc)
        h = num / jnp.maximum(jnp.abs(den), jnp.exp(-m_t))[..., None]
        m_new = m_t[:, -1]
        ws = jnp.exp(b[:, -1:, :] - b + igc - m_new[:, None, :])
        carry_w = jnp.exp(b[:, -1] + m - m_new)
        C = carry_w[..., None, None] * C + jnp.einsum('bsh,bshv,bshk->bhvk', ws, vc, kc)
        n = carry_w[..., None] * n + jnp.einsum('bsh,bshk->bhk', ws, kc)
        return (C, n, m_new), h

    (C, n, m), hs = lax.scan(step, (C0.astype(f32), n0.astype(f32), m0.astype(f32)),
                             (chunks(q), chunks(k), chunks(v), chunks(ig), chunks(lf)))
    return hs.swapaxes(0, 1).reshape(B, T, H, d), C, n, m


def mlstm(q_raw, k_raw, v, o_raw, i_raw, f_raw, conv_prev, C0, n0, m0, conv_w, conv_b, b_i, b_f, norm_w):
    B, T, _ = q_raw.shape
    xp = jnp.concatenate([conv_prev, jnp.concatenate([q_raw, k_raw], axis=-1)], axis=1)
    conv = conv_b
    for j in range(MLSTM_CONV):
        conv = conv + xp[:, j:j + T] * conv_w[j]
    qk = jax.nn.silu(conv)

    def hd(z):
        return z.reshape(B, T, MLSTM_HEADS, MLSTM_HD)

    q = hd(qk[..., :BRANCH_W])
    k = hd(qk[..., BRANCH_W:]) * (MLSTM_HD ** -0.5)
    ig = (i_raw + b_i).astype(jnp.float32)
    lf = jax.nn.log_sigmoid((f_raw + b_f).astype(jnp.float32))
    h, C, n, m = mlstm_chunkwise(q, k, hd(v), ig, lf, C0, n0, m0)
    h = headnorm(h, EPS).reshape(B, T, BRANCH_W) * norm_w.astype(jnp.float32)
    out = jax.nn.sigmoid(o_raw.astype(jnp.float32)) * h
    dt = q_raw.dtype
    return out.astype(dt), xp[:, T:], C.astype(dt), n.astype(dt), m.astype(dt)


def moe(h, router_w, router_b, w1, b1, w2, b2):
    B, T, D = h.shape
    x = h.reshape(B * T, D)
    logits = (x @ router_w + router_b).astype(jnp.float32)
    top_v, top_i = lax.top_k(logits, TOP_K)
    top_w = jax.nn.softmax(top_v, axis=-1)
    gate = jnp.sum(jax.nn.one_hot(top_i, N_EXPERTS, dtype=jnp.float32) * top_w[..., None], axis=1).astype(x.dtype)
    out = jnp.zeros_like(x)
    for e in range(N_EXPERTS):
        gu = x @ w1[e] + b1[e]
        g_ = jnp.minimum(gu[:, :EXPERT_FF], SWIGLU_LIMIT)
        u_ = jnp.clip(gu[:, EXPERT_FF:], -SWIGLU_LIMIT, SWIGLU_LIMIT)
        act = (u_ + 1.0) * g_ * jax.nn.sigmoid(SWIGLU_ALPHA * g_)
        out = out + gate[:, e:e + 1] * (act @ w2[e] + b2[e])
    return out.reshape(B, T, D)


def trunk_layer(x, c, st, p):
    mod = jax.nn.silu(c) @ p['w_ada'] + p['b_ada']
    sh1, sc1, g1, sh2, sc2, g2 = jnp.split(mod[:, None, :], 6, axis=-1)
    h = rmsnorm(x, p['norm1']) * (1.0 + sc1) + sh1
    u = h @ p['w_in']
    u_mla, u_rw, u_fox, u_ml, u_gate = _split(u, [MLA_IN, RWKV_IN, FOX_IN, MLSTM_IN, GATE_IN])
    cq, ckv, kr = _split(u_mla, [MLA_Q_RANK, MLA_KV_RANK, MLA_ROPE])
    o_mla, lat_new, kr_new = mla(cq, ckv, kr, st['mla_ckv'], st['mla_krope'], p['mla_q_norm'],
                                 p['mla_kv_norm'], p['mla_w_uq'], p['mla_w_ukv'])
    o_rw, s_new, shift_new = rwkv7(u_rw, st['rwkv_shift'], st['rwkv_S'], p['rwkv_mu'], p['rwkv_w0'],
                                   p['rwkv_w_lora'], p['rwkv_a0'], p['rwkv_a_lora'], p['rwkv_g_lora'],
                                   p['rwkv_k_k'], p['rwkv_k_a'], p['rwkv_r_k'], p['rwkv_gn_w'], p['rwkv_gn_b'])
    fq, fk, fv, ff = _split(u_fox, [BRANCH_W] * 3 + [FOX_HEADS])
    o_fox, fk_new, fv_new, lf_new = fox(fq, fk, fv, ff, st['fox_k'], st['fox_v'], st['fox_logf'], p['fox_b_f'])
    mq, mk, mv, mo, mi, mf = _split(u_ml, [BRANCH_W] * 4 + [MLSTM_HEADS] * 2)
    o_ml, conv_new, C_new, n_new, m_new = mlstm(mq, mk, mv, mo, mi, mf, st['mlstm_conv'], st['mlstm_C'],
                                                st['mlstm_n'], st['mlstm_m'], p['mlstm_conv_w'],
                                                p['mlstm_conv_b'], p['mlstm_b_i'], p['mlstm_b_f'],
                                                p['mlstm_norm_w'])
    mixed = jnp.zeros_like(x)
    for br, o_br in enumerate((o_mla, o_rw, o_fox, o_ml)):
        gate = jax.nn.sigmoid(u_gate[..., br * D_MODEL:(br + 1) * D_MODEL])
        mixed = mixed + gate * (o_br @ p['w_branch'][br])
    x = x + g1 * (mixed @ p['w_out'])
    h2 = rmsnorm(x, p['norm2']) * (1.0 + sc2) + sh2
    x = x + g2 * moe(h2, p['router_w'], p['router_b'], p['expert_w1'], p['expert_b1'],
                     p['expert_w2'], p['expert_b2'])
    new = {'mla_ckv': lat_new, 'mla_krope': kr_new, 'fox_k': fk_new, 'fox_v': fv_new, 'fox_logf': lf_new,
           'rwkv_S': s_new, 'rwkv_shift': shift_new, 'mlstm_C': C_new, 'mlstm_n': n_new,
           'mlstm_m': m_new, 'mlstm_conv': conv_new}
    return x, new


def setup_inputs(seed: int = 0) -> dict:
    key = jax.random.key(seed)
    ks = iter(jax.random.split(key, 64))

    def nrm(shape, scale=1.0, mean=0.0):
        return mean + scale * jax.random.normal(next(ks), shape, jnp.float32)

    W, Dp = BRANCH_W, DEPTH
    return {
        'x_prompt': nrm((BATCH, SEQ, D_MODEL)),
        'x_sample': nrm((DEC_BATCH, DEC_SEQ, D_MODEL)),
        'c_prompt': nrm((BATCH, D_MODEL)),
        'c_sample': nrm((DEC_BATCH, D_MODEL)),
        'cache_mla_ckv': nrm((Dp, DEC_BATCH, PAST_LEN, MLA_KV_RANK)),
        'cache_mla_krope': nrm((Dp, DEC_BATCH, PAST_LEN, MLA_ROPE)),
        'cache_fox_k': nrm((Dp, DEC_BATCH, PAST_LEN, FOX_HEADS, FOX_HD)),
        'cache_fox_v': nrm((Dp, DEC_BATCH, PAST_LEN, FOX_HEADS, FOX_HD)),
        'cache_fox_logf': jax.nn.log_sigmoid(nrm((Dp, DEC_BATCH, PAST_LEN, FOX_HEADS), 0.5, 2.0)),
        'state_rwkv_S': nrm((Dp, DEC_BATCH, RWKV_HEADS, RWKV_HD, RWKV_HD), 0.1),
        'state_rwkv_shift': nrm((Dp, DEC_BATCH, RWKV_IN)),
        'state_mlstm_C': nrm((Dp, DEC_BATCH, MLSTM_HEADS, MLSTM_HD, MLSTM_HD), 0.1),
        'state_mlstm_n': nrm((Dp, DEC_BATCH, MLSTM_HEADS, MLSTM_HD), 0.1),
        'state_mlstm_m': nrm((Dp, DEC_BATCH, MLSTM_HEADS)),
        'state_mlstm_conv': nrm((Dp, DEC_BATCH, MLSTM_CONV - 1, 2 * W)),
        'w_ada': nrm((Dp, D_MODEL, 6 * D_MODEL), 0.5 * D_MODEL ** -0.5),
        'b_ada': nrm((Dp, 6 * D_MODEL), 0.01),
        'norm1': nrm((Dp, D_MODEL), 0.05, 1.0),
        'norm2': nrm((Dp, D_MODEL), 0.05, 1.0),
        'w_in': nrm((Dp, D_MODEL, IN_W), D_MODEL ** -0.5),
        'mla_q_norm': nrm((Dp, MLA_Q_RANK), 0.05, 1.0),
        'mla_kv_norm': nrm((Dp, MLA_KV_RANK), 0.05, 1.0),
        'mla_w_uq': nrm((Dp, MLA_Q_RANK, MLA_HEADS * (MLA_NOPE + MLA_ROPE)), MLA_Q_RANK ** -0.5),
        'mla_w_ukv': nrm((Dp, MLA_KV_RANK, MLA_HEADS * (MLA_NOPE + MLA_V)), MLA_KV_RANK ** -0.5),
        'rwkv_mu': jax.random.uniform(next(ks), (Dp, RWKV_IN), jnp.float32),
        'rwkv_w0': nrm((Dp, W), 0.5, -1.0),
        'rwkv_w_lora': nrm((Dp, RWKV_W_LORA, W), 0.5 * RWKV_W_LORA ** -0.5),
        'rwkv_a0': nrm((Dp, W), 0.1),
        'rwkv_a_lora': nrm((Dp, RWKV_A_LORA, W), 0.5 * RWKV_A_LORA ** -0.5),
        'rwkv_g_lora': nrm((Dp, RWKV_G_LORA, W), RWKV_G_LORA ** -0.5),
        'rwkv_k_k': nrm((Dp, W), 0.1, 1.0),
        'rwkv_k_a': nrm((Dp, W), 0.1, 1.0),
        'rwkv_r_k': nrm((Dp, RWKV_HEADS, RWKV_HD), 0.1),
        'rwkv_gn_w': nrm((Dp, W), 0.05, 1.0),
        'rwkv_gn_b': nrm((Dp, W), 0.01),
        'fox_b_f': nrm((Dp, FOX_HEADS), 0.5, 2.0),
        'mlstm_conv_w': nrm((Dp, MLSTM_CONV, 2 * W), 0.5),
        'mlstm_conv_b': nrm((Dp, 2 * W), 0.01),
        'mlstm_b_i': nrm((Dp, MLSTM_HEADS), 0.1),
        'mlstm_b_f': nrm((Dp, MLSTM_HEADS), 0.5, 3.0),
        'mlstm_norm_w': nrm((Dp, W), 0.05, 1.0),
        'w_branch': nrm((Dp, N_BRANCH, W, D_MODEL), W ** -0.5),
        'w_out': nrm((Dp, D_MODEL, D_MODEL), D_MODEL ** -0.5),
        'router_w': nrm((Dp, D_MODEL, N_EXPERTS), D_MODEL ** -0.5),
        'router_b': nrm((Dp, N_EXPERTS), 0.01),
        'expert_w1': nrm((Dp, N_EXPERTS, D_MODEL, 2 * EXPERT_FF), D_MODEL ** -0.5),
        'expert_b1': nrm((Dp, N_EXPERTS, 2 * EXPERT_FF), 0.01),
        'expert_w2': nrm((Dp, N_EXPERTS, EXPERT_FF, D_MODEL), EXPERT_FF ** -0.5),
        'expert_b2': nrm((Dp, N_EXPERTS, D_MODEL), 0.01),
        'final_norm': nrm((D_MODEL,), 0.05, 1.0),
    }


def reference(x_prompt, x_sample, c_prompt, c_sample,
              cache_mla_ckv, cache_mla_krope, cache_fox_k, cache_fox_v, cache_fox_logf,
              state_rwkv_S, state_rwkv_shift, state_mlstm_C, state_mlstm_n, state_mlstm_m, state_mlstm_conv,
              w_ada, b_ada, norm1, norm2, w_in,
              mla_q_norm, mla_kv_norm, mla_w_uq, mla_w_ukv,
              rwkv_mu, rwkv_w0, rwkv_w_lora, rwkv_a0, rwkv_a_lora, rwkv_g_lora, rwkv_k_k, rwkv_k_a,
              rwkv_r_k, rwkv_gn_w, rwkv_gn_b,
              fox_b_f,
              mlstm_conv_w, mlstm_conv_b, mlstm_b_i, mlstm_b_f, mlstm_norm_w,
              w_branch, w_out, router_w, router_b, expert_w1, expert_b1, expert_w2, expert_b2, final_norm):
    params = dict(w_ada=w_ada, b_ada=b_ada, norm1=norm1, norm2=norm2, w_in=w_in,
                  mla_q_norm=mla_q_norm, mla_kv_norm=mla_kv_norm, mla_w_uq=mla_w_uq, mla_w_ukv=mla_w_ukv,
                  rwkv_mu=rwkv_mu, rwkv_w0=rwkv_w0, rwkv_w_lora=rwkv_w_lora, rwkv_a0=rwkv_a0,
                  rwkv_a_lora=rwkv_a_lora, rwkv_g_lora=rwkv_g_lora, rwkv_k_k=rwkv_k_k, rwkv_k_a=rwkv_k_a,
                  rwkv_r_k=rwkv_r_k, rwkv_gn_w=rwkv_gn_w, rwkv_gn_b=rwkv_gn_b, fox_b_f=fox_b_f,
                  mlstm_conv_w=mlstm_conv_w, mlstm_conv_b=mlstm_conv_b, mlstm_b_i=mlstm_b_i,
                  mlstm_b_f=mlstm_b_f, mlstm_norm_w=mlstm_norm_w, w_branch=w_branch, w_out=w_out,
                  router_w=router_w, router_b=router_b, expert_w1=expert_w1, expert_b1=expert_b1,
                  expert_w2=expert_w2, expert_b2=expert_b2)

    def run_group(x, c, past):
        new = {}
        for l in range(DEPTH):
            p = {name: arr[l] for name, arr in params.items()}
            st = {name: arr[l] for name, arr in past.items()}
            x, ns = trunk_layer(x, c, st, p)
            for name, arr in ns.items():
                new.setdefault(name, []).append(arr)
        return rmsnorm(x, final_norm), {name: jnp.stack(arrs) for name, arrs in new.items()}

    Bp = x_prompt.shape[0]
    dt = x_prompt.dtype
    prompt_past = dict(
        mla_ckv=jnp.zeros((DEPTH, Bp, 0, MLA_KV_RANK), dt),
        mla_krope=jnp.zeros((DEPTH, Bp, 0, MLA_ROPE), dt),
        fox_k=jnp.zeros((DEPTH, Bp, 0, FOX_HEADS, FOX_HD), dt),
        fox_v=jnp.zeros((DEPTH, Bp, 0, FOX_HEADS, FOX_HD), dt),
        fox_logf=jnp.zeros((DEPTH, Bp, 0, FOX_HEADS), dt),
        rwkv_S=jnp.zeros((DEPTH, Bp, RWKV_HEADS, RWKV_HD, RWKV_HD), dt),
        rwkv_shift=jnp.zeros((DEPTH, Bp, RWKV_IN), dt),
        mlstm_C=jnp.zeros((DEPTH, Bp, MLSTM_HEADS, MLSTM_HD, MLSTM_HD), dt),
        mlstm_n=jnp.zeros((DEPTH, Bp, MLSTM_HEADS, MLSTM_HD), dt),
        mlstm_m=jnp.zeros((DEPTH, Bp, MLSTM_HEADS), dt),
        mlstm_conv=jnp.zeros((DEPTH, Bp, MLSTM_CONV - 1, 2 * BRANCH_W), dt))
    sample_past = dict(
        mla_ckv=cache_mla_ckv, mla_krope=cache_mla_krope, fox_k=cache_fox_k, fox_v=cache_fox_v,
        fox_logf=cache_fox_logf, rwkv_S=state_rwkv_S, rwkv_shift=state_rwkv_shift,
        mlstm_C=state_mlstm_C, mlstm_n=state_mlstm_n, mlstm_m=state_mlstm_m, mlstm_conv=state_mlstm_conv)
    y_prompt, sp = run_group(x_prompt, c_prompt, prompt_past)
    y_sample, ss = run_group(x_sample, c_sample, sample_past)
    return (y_prompt, y_sample,
            sp['mla_ckv'], ss['mla_ckv'], sp['mla_krope'], ss['mla_krope'],
            sp['fox_k'], ss['fox_k'], sp['fox_v'], ss['fox_v'], sp['fox_logf'], ss['fox_logf'],
            sp['rwkv_S'], ss['rwkv_S'], sp['rwkv_shift'], ss['rwkv_shift'],
            sp['mlstm_C'], ss['mlstm_C'], sp['mlstm_n'], ss['mlstm_n'], sp['mlstm_m'], ss['mlstm_m'],
            sp['mlstm_conv'], ss['mlstm_conv'])
```

```python
import functools

import jax
import jax.numpy as jnp
from jax import lax
from jax.experimental import pallas as pl
from jax.experimental.pallas import tpu as pltpu

D_MODEL = 4096
DEPTH = 2
CHUNK = 64
Q_BLOCK = 128
N_BRANCH = 4
BRANCH_W = D_MODEL // N_BRANCH
MLA_HEADS = 8
MLA_NOPE = 128
MLA_ROPE = 64
MLA_V = BRANCH_W // MLA_HEADS
MLA_Q_RANK = 768
MLA_KV_RANK = 512
ROPE_THETA = 10000.0
RWKV_HD = 64
RWKV_HEADS = BRANCH_W // RWKV_HD
RWKV_W_LORA = 64
RWKV_A_LORA = 64
RWKV_G_LORA = 128
RWKV_GN_EPS = 64e-5
RWKV_IN = 3 * BRANCH_W + RWKV_W_LORA + RWKV_A_LORA + RWKV_G_LORA
FOX_HEADS = 8
FOX_HD = BRANCH_W // FOX_HEADS
MLSTM_HEADS = 8
MLSTM_HD = BRANCH_W // MLSTM_HEADS
MLSTM_CONV = 4
N_EXPERTS = 32
TOP_K = 4
EXPERT_FF = D_MODEL // 2
SWIGLU_LIMIT = 7.0
SWIGLU_ALPHA = 1.702
EPS = 1e-6
NEG_INF = -1e30
MLA_IN = MLA_Q_RANK + MLA_KV_RANK + MLA_ROPE
FOX_IN = 3 * BRANCH_W + FOX_HEADS
MLSTM_IN = 4 * BRANCH_W + 2 * MLSTM_HEADS
GATE_IN = N_BRANCH * D_MODEL
IN_W = MLA_IN + RWKV_IN + FOX_IN + MLSTM_IN + GATE_IN

V7X_LANES = 128
V7X_VMEM_LIMIT_BYTES = 56 * 1024 * 1024
BF16 = jnp.bfloat16
F32 = jnp.float32


def _tile(n, pref):
    t = pref
    while t >= 8:
        if n % t == 0:
            return t
        t //= 2
    return n


def _cparams(sem):
    return pltpu.CompilerParams(dimension_semantics=sem, vmem_limit_bytes=V7X_VMEM_LIMIT_BYTES)


def _mm_kernel(a_ref, w_ref, o_ref):
    o_ref[...] = jnp.dot(a_ref[...], w_ref[...], preferred_element_type=F32)


def _mm(a, w, tm_pref=1024, tn_pref=512):
    M, K = a.shape
    N = w.shape[1]
    tm, tn = _tile(M, tm_pref), _tile(N, tn_pref)
    return pl.pallas_call(
        _mm_kernel,
        out_shape=jax.ShapeDtypeStruct((M, N), F32),
        grid=(M // tm, N // tn),
        in_specs=[pl.BlockSpec((tm, K), lambda i, j: (i, 0)),
                  pl.BlockSpec((K, tn), lambda i, j: (0, j))],
        out_specs=pl.BlockSpec((tm, tn), lambda i, j: (i, j)),
        compiler_params=_cparams(("parallel", "parallel")),
        name="dense_mm",
    )(a, w)


def _merge_kernel(o_ref, wb_ref, ug_ref, out_ref, acc_ref):
    b = pl.program_id(2)

    @pl.when(b == 0)
    def _():
        acc_ref[...] = jnp.zeros_like(acc_ref)

    acc_ref[...] += jax.nn.sigmoid(ug_ref[...]) * jnp.dot(o_ref[0], wb_ref[0], preferred_element_type=F32)

    @pl.when(b == pl.num_programs(2) - 1)
    def _():
        out_ref[...] = acc_ref[...].astype(out_ref.dtype)


def _merge(o_all, wb, u, tm_pref=1024, tn_pref=1024):
    nb, M, W = o_all.shape
    D = wb.shape[2]
    tm, tn = _tile(M, tm_pref), _tile(D, tn_pref)
    npb = D // tn
    return pl.pallas_call(
        _merge_kernel,
        out_shape=jax.ShapeDtypeStruct((M, D), BF16),
        grid=(M // tm, npb, nb),
        in_specs=[pl.BlockSpec((1, tm, W), lambda i, j, b: (b, i, 0)),
                  pl.BlockSpec((1, W, tn), lambda i, j, b: (b, 0, j)),
                  pl.BlockSpec((tm, tn), lambda i, j, b: (i, b * npb + j))],
        out_specs=pl.BlockSpec((tm, tn), lambda i, j, b: (i, j)),
        scratch_shapes=[pltpu.VMEM((tm, tn), F32)],
        compiler_params=_cparams(("parallel", "parallel", "arbitrary")),
        name="branch_merge",
    )(o_all, wb, u)


def _rwkv_scan_kernel(r_ref, w_ref, k_ref, v_ref, kk_ref, a_ref, s0_ref, y_ref, s_ref, kkn_scr, b_scr):
    hd = s_ref.shape[0]
    steps = r_ref.shape[0]
    n_acc = 4

    @pl.when(pl.program_id(1) == 0)
    def _():
        s_ref[...] = s0_ref[...]

    def step(t, carry):
        kkraw = kk_ref[t]
        nrm = jnp.sqrt(jnp.sum(kkraw * kkraw, axis=0, keepdims=True))
        kkn = kkraw / jnp.maximum(nrm, 1e-12)
        kkn_scr[...] = kkn
        b_scr[...] = kkn * a_ref[t]
        acc = [None] * n_acc
        for j in range(hd):
            term = s_ref[j] * kkn_scr[pl.ds(j, 1), :]
            acc[j % n_acc] = term if acc[j % n_acc] is None else acc[j % n_acc] + term
        sa = -functools.reduce(lambda x, y: x + y, acc)
        v = v_ref[t]
        yacc = [None] * n_acc
        for j in range(hd):
            sj = (s_ref[j] * w_ref[t, pl.ds(j, 1), :] + sa * b_scr[pl.ds(j, 1), :]
                  + v * k_ref[t, pl.ds(j, 1), :])
            s_ref[j] = sj
            term = sj * r_ref[t, pl.ds(j, 1), :]
            yacc[j % n_acc] = term if yacc[j % n_acc] is None else yacc[j % n_acc] + term
        y_ref[t] = functools.reduce(lambda x, y: x + y, yacc)
        return carry

    lax.fori_loop(0, steps, step, 0)


def _rwkv_scan(r, w, k, v, kk, a, s0, tb_pref=32):
    T, hd, NL = r.shape
    tb = _tile(T, tb_pref)
    seq = pl.BlockSpec((tb, hd, V7X_LANES), lambda g, t: (t, 0, g))
    st = pl.BlockSpec((hd, hd, V7X_LANES), lambda g, t: (0, 0, g))
    return pl.pallas_call(
        _rwkv_scan_kernel,
        out_shape=(jax.ShapeDtypeStruct((T, hd, NL), F32), jax.ShapeDtypeStruct((hd, hd, NL), F32)),
        grid=(NL // V7X_LANES, T // tb),
        in_specs=[seq] * 6 + [st],
        out_specs=(seq, st),
        scratch_shapes=[pltpu.VMEM((hd, V7X_LANES), F32), pltpu.VMEM((hd, V7X_LANES), F32)],
        compiler_params=_cparams(("parallel", "arbitrary")),
        name="rwkv_scan",
    )(r, w, k, v, kk, a, s0)


def _expert_kernel(te_ref, tv_ref, x_ref, w1g_ref, w1u_ref, b1g_ref, b1u_ref, w2_ref, b2_ref, rw_ref, o_ref):
    i, j = pl.program_id(0), pl.program_id(1)
    valid = tv_ref[i] == 1

    @pl.when(valid)
    def _():
        x = x_ref[...]
        g = jnp.dot(x, w1g_ref[0].astype(BF16), preferred_element_type=F32) + b1g_ref[0]
        u = jnp.dot(x, w1u_ref[0].astype(BF16), preferred_element_type=F32) + b1u_ref[0]
        g = jnp.minimum(g, SWIGLU_LIMIT)
        u = jnp.clip(u, -SWIGLU_LIMIT, SWIGLU_LIMIT)
        act = (u + 1.0) * g * jax.nn.sigmoid(SWIGLU_ALPHA * g)
        contrib = jnp.dot(act.astype(BF16), w2_ref[0].astype(BF16), preferred_element_type=F32)

        @pl.when(j == 0)
        def _():
            o_ref[...] = contrib + b2_ref[0]

        @pl.when(j > 0)
        def _():
            o_ref[...] += contrib

        @pl.when(j == pl.num_programs(1) - 1)
        def _():
            o_ref[...] = o_ref[...] * rw_ref[...]

    @pl.when(jnp.logical_and(jnp.logical_not(valid), j == 0))
    def _():
        o_ref[...] = jnp.zeros_like(o_ref)


def _experts(xg, w1, b1, w2, b2, roww, tile_e, tile_v, tm, tf_pref=256):
    R, D = xg.shape
    E, F = w2.shape[0], w2.shape[1]
    tf = _tile(F, tf_pref)
    nf = F // tf

    def jj(i, j, tv):
        return jnp.where(tv[i] == 1, j, nf - 1)

    grid_spec = pltpu.PrefetchScalarGridSpec(
        num_scalar_prefetch=2,
        grid=(R // tm, nf),
        in_specs=[
            pl.BlockSpec((tm, D), lambda i, j, te, tv: (i, 0)),
            pl.BlockSpec((1, D, tf), lambda i, j, te, tv: (te[i], 0, jj(i, j, tv))),
            pl.BlockSpec((1, D, tf), lambda i, j, te, tv: (te[i], 0, nf + jj(i, j, tv))),
            pl.BlockSpec((1, 1, tf), lambda i, j, te, tv: (te[i], 0, jj(i, j, tv))),
            pl.BlockSpec((1, 1, tf), lambda i, j, te, tv: (te[i], 0, nf + jj(i, j, tv))),
            pl.BlockSpec((1, tf, D), lambda i, j, te, tv: (te[i], jj(i, j, tv), 0)),
            pl.BlockSpec((1, 1, D), lambda i, j, te, tv: (te[i], 0, 0)),
            pl.BlockSpec((tm, 1), lambda i, j, te, tv: (i, 0)),
        ],
        out_specs=pl.BlockSpec((tm, D), lambda i, j, te, tv: (i, 0), pipeline_mode=pl.Buffered(1)),
    )
    return pl.pallas_call(
        _expert_kernel,
        out_shape=jax.ShapeDtypeStruct((R, D), F32),
        grid_spec=grid_spec,
        compiler_params=_cparams(("parallel", "arbitrary")),
        name="moe_experts",
    )(tile_e, tile_v, xg, w1, w1, b1, b1, w2, b2, roww)


def _moe(x, router_w, router_b, w1, b1, w2, b2, layer, tm_pref=512):
    N, D = x.shape
    E = router_w.shape[1]
    w1 = w1.reshape((-1,) + w1.shape[2:])
    w2 = w2.reshape((-1,) + w2.shape[2:])
    b1 = b1.reshape(-1, 1, b1.shape[-1])
    b2 = b2.reshape(-1, 1, b2.shape[-1])
    logits = jnp.dot(x.astype(F32), router_w, precision=lax.Precision.HIGHEST) + router_b
    top_v, top_i = lax.top_k(logits, TOP_K)
    top_w = jax.nn.softmax(top_v, axis=-1)
    npair = N * TOP_K
    tm = _tile(npair, tm_pref)
    R = npair + E * tm
    e_flat = top_i.reshape(-1).astype(jnp.int32)
    onehot = (e_flat[:, None] == jnp.arange(E, dtype=jnp.int32)[None, :]).astype(jnp.int32)
    csum = jnp.cumsum(onehot, axis=0)
    counts = csum[-1]
    rank = jnp.sum((csum - onehot) * onehot, axis=1)
    starts = jnp.cumsum(counts) - counts
    padded = ((counts + tm - 1) // tm) * tm
    pend = jnp.cumsum(padded)
    pstart = pend - padded
    pos = pstart[e_flat] + rank
    order = jnp.argsort(e_flat, stable=True).astype(jnp.int32)
    rows = jnp.arange(R, dtype=jnp.int32)
    row_e = jnp.minimum(jnp.searchsorted(pend, rows, side="right"), E - 1).astype(jnp.int32)
    row_rank = rows - pstart[row_e]
    row_ok = jnp.logical_and(rows < pend[-1], row_rank < counts[row_e])
    row_pair = order[jnp.clip(starts[row_e] + row_rank, 0, npair - 1)]
    src_tok = jnp.where(row_ok, row_pair // TOP_K, 0)
    roww = jnp.where(row_ok, top_w.reshape(-1)[row_pair], 0.0)
    tile_start = jnp.arange(R // tm, dtype=jnp.int32) * tm
    tile_e = row_e[::tm]
    tile_v = (tile_start < pend[-1]).astype(jnp.int32)
    last_e = tile_e[jnp.maximum(jnp.sum(tile_v) - 1, 0)]
    tile_e = jnp.where(tile_v == 1, tile_e, last_e) + layer * E
    xg = jnp.take(x, src_tok, axis=0)
    y = _experts(xg, w1, b1, w2, b2, roww[:, None], tile_e, tile_v, tm)
    return jnp.sum(jnp.take(y, pos, axis=0).reshape(N, TOP_K, D), axis=1)


def _attn_kernel(*refs, scale, tk, q_off, tk_valid, chunk_mask, use_bias):
    if use_bias:
        q_ref, k_ref, v_ref, fq_ref, fk_ref, o_ref, m_scr, l_scr, acc_scr = refs
    else:
        q_ref, k_ref, v_ref, o_ref, m_scr, l_scr, acc_scr = refs
    tq = q_ref.shape[2]
    n_kb_total = k_ref.shape[2] // tk
    qi = pl.program_id(2)
    q = q_ref[0, 0]
    m_scr[...] = jnp.full_like(m_scr, NEG_INF)
    l_scr[...] = jnp.zeros_like(l_scr)
    acc_scr[...] = jnp.zeros_like(acc_scr)
    q_pos = q_off + qi * tq + lax.broadcasted_iota(jnp.int32, (tq, tk), 0)
    last_q = q_off + (qi + 1) * tq - 1
    limit = (last_q // CHUNK + 1) * CHUNK if chunk_mask else last_q + 1
    n_kb = jnp.minimum((limit + tk - 1) // tk, n_kb_total)

    def body(kb, carry):
        start = pl.multiple_of(kb * tk, tk)
        k = k_ref[0, 0, pl.ds(start, tk), :]
        v = v_ref[0, 0, pl.ds(start, tk), :]
        s = lax.dot_general(q, k, (((1,), (1,)), ((), ())), preferred_element_type=F32) * scale
        if use_bias:
            s = s + (fq_ref[0, 0] - fk_ref[0, 0, kb])
        k_pos = kb * tk + lax.broadcasted_iota(jnp.int32, (tq, tk), 1)
        ok = (k_pos // CHUNK <= q_pos // CHUNK) if chunk_mask else (k_pos <= q_pos)
        ok = jnp.logical_and(ok, k_pos < tk_valid)
        s = jnp.where(ok, s, NEG_INF)
        m_old = m_scr[...]
        m_new = jnp.maximum(m_old, jnp.max(s, axis=1, keepdims=True))
        alpha = jnp.exp(m_old - m_new)
        p = jnp.exp(s - m_new)
        l_scr[...] = alpha * l_scr[...] + jnp.sum(p, axis=1, keepdims=True)
        acc_scr[...] = alpha * acc_scr[...] + jnp.dot(p.astype(v.dtype), v, preferred_element_type=F32)
        m_scr[...] = m_new
        return carry

    lax.fori_loop(0, n_kb, body, 0)
    o_ref[0, 0] = acc_scr[...] / l_scr[...]


def _attention(q, k, v, *, tk_valid, chunk_mask, fq=None, fk=None, tq_pref=512):
    B, H, Tq, dq = q.shape
    Tk, dv = k.shape[2], v.shape[3]
    tq = _tile(Tq, tq_pref)
    tk = Tk if tq * Tk * 4 <= 2 * 1024 * 1024 else _tile(Tk, 512)
    use_bias = fq is not None
    in_specs = [pl.BlockSpec((1, 1, tq, dq), lambda b, h, i: (b, h, i, 0)),
                pl.BlockSpec((1, 1, Tk, dq), lambda b, h, i: (b, h, 0, 0)),
                pl.BlockSpec((1, 1, Tk, dv), lambda b, h, i: (b, h, 0, 0))]
    args = [q, k, v]
    if use_bias:
        in_specs += [pl.BlockSpec((1, 1, tq, 1), lambda b, h, i: (b, h, i, 0)),
                     pl.BlockSpec((1, 1, Tk // tk, 1, tk), lambda b, h, i: (b, h, 0, 0, 0))]
        args += [fq[..., None], fk.reshape(B, H, Tk // tk, 1, tk)]
    return pl.pallas_call(
        functools.partial(_attn_kernel, scale=dq ** -0.5, tk=tk, q_off=tk_valid - Tq, tk_valid=tk_valid,
                          chunk_mask=chunk_mask, use_bias=use_bias),
        out_shape=jax.ShapeDtypeStruct((B, H, Tq, dv), F32),
        grid=(B, H, Tq // tq),
        in_specs=in_specs,
        out_specs=pl.BlockSpec((1, 1, tq, dv), lambda b, h, i: (b, h, i, 0)),
        scratch_shapes=[pltpu.VMEM((tq, 1), F32), pltpu.VMEM((tq, 1), F32), pltpu.VMEM((tq, dv), F32)],
        compiler_params=_cparams(("parallel", "parallel", "arbitrary")),
        name="causal_attention",
    )(*args)


def _mlstm_kernel(q_ref, k_ref, v_ref, o_ref, bcol_ref, brow_ref, igcol_ref, igrow_ref, c0_ref, n0_ref, m0_ref,
                  nw_ref, out_ref, c_ref, n_ref, m_ref):
    H, d = c_ref.shape[1], c_ref.shape[2]
    L = q_ref.shape[1]

    @pl.when(pl.program_id(1) == 0)
    def _():
        c_ref[...] = c0_ref[...]
        n_ref[...] = n0_ref[...]
        m_ref[...] = m0_ref[...]

    tri = lax.broadcasted_iota(jnp.int32, (L, L), 1) <= lax.broadcasted_iota(jnp.int32, (L, L), 0)
    nt = (((1,), (1,)), ((), ()))
    for h in range(H):
        sl = slice(h * d, (h + 1) * d)
        q, k, v = q_ref[0, :, sl], k_ref[0, :, sl], v_ref[0, :, sl]
        bcol, brow = bcol_ref[0, h], brow_ref[0, h, 0]
        igcol, igrow = igcol_ref[0, h], igrow_ref[0, h, 0]
        mp = m_ref[0, h][:, :1]
        dlog = jnp.where(tri, bcol - brow + igrow, NEG_INF)
        inter = bcol + mp
        m_t = jnp.maximum(inter, jnp.max(dlog, axis=1, keepdims=True))
        dw = jnp.exp(dlog - m_t)
        iw = jnp.exp(inter - m_t)
        qb, kb, vb = q.astype(BF16), k.astype(BF16), v.astype(BF16)
        sw = lax.dot_general(qb, kb, nt, preferred_element_type=F32) * dw
        C, n = c_ref[0, h], n_ref[0, h]
        num = (jnp.dot(sw.astype(BF16), vb, preferred_element_type=F32)
               + iw * lax.dot_general(qb, C.astype(BF16), nt, preferred_element_type=F32))
        den = jnp.sum(sw, axis=1, keepdims=True) + iw * jnp.sum(q * n, axis=1, keepdims=True)
        hh = num / jnp.maximum(jnp.abs(den), jnp.exp(-m_t))
        m_new = m_t[L - 1:L, :]
        b_last = bcol[L - 1:L, :]
        wscol = jnp.exp(b_last - bcol + igcol - m_new)
        cw = jnp.exp(b_last + mp - m_new)
        wv_t = jnp.transpose(wscol * v).astype(BF16)
        c_ref[0, h] = cw * C + jnp.dot(wv_t, kb, preferred_element_type=F32)
        n_ref[0, h] = cw * n + jnp.sum(wscol * k, axis=0, keepdims=True)
        m_ref[0, h] = jnp.broadcast_to(m_new, (1, d))
        mu = jnp.mean(hh, axis=1, keepdims=True)
        var = jnp.mean(jnp.square(hh - mu), axis=1, keepdims=True)
        y = (hh - mu) * lax.rsqrt(var + EPS) * nw_ref[:, sl]
        out_ref[0, :, sl] = jax.nn.sigmoid(o_ref[0, :, sl]) * y


def _mlstm_chunks(q, k, v, o_raw, ig, lf, C0, n0, m0, norm_w):
    B, T, W = q.shape
    H = ig.shape[2]
    d = W // H
    L = CHUNK if T % CHUNK == 0 else T
    nc = T // L
    b = jnp.cumsum(lf.reshape(B, nc, L, H), axis=2).reshape(B, T, H)
    bt, igt = jnp.swapaxes(b, 1, 2), jnp.swapaxes(ig, 1, 2)
    seq = pl.BlockSpec((1, L, W), lambda i, c: (i, c, 0))
    col = pl.BlockSpec((1, H, L, 1), lambda i, c: (i, 0, c, 0))
    row = pl.BlockSpec((1, H, 1, 1, L), lambda i, c: (i, 0, c, 0, 0))
    rows = [bt.reshape(B, H, nc, 1, L), igt.reshape(B, H, nc, 1, L)]
    stC = pl.BlockSpec((1, H, d, d), lambda i, c: (i, 0, 0, 0))
    stn = pl.BlockSpec((1, H, 1, d), lambda i, c: (i, 0, 0, 0))
    m0b = jnp.broadcast_to(m0[:, :, None, None], (B, H, 1, d))
    out, C, n, m = pl.pallas_call(
        _mlstm_kernel,
        out_shape=(jax.ShapeDtypeStruct((B, T, W), F32), jax.ShapeDtypeStruct((B, H, d, d), F32),
                   jax.ShapeDtypeStruct((B, H, 1, d), F32), jax.ShapeDtypeStruct((B, H, 1, d), F32)),
        grid=(B, nc),
        in_specs=[seq, seq, seq, seq, col, row, col, row, stC, stn, stn,
                  pl.BlockSpec((1, W), lambda i, c: (0, 0))],
        out_specs=(seq, stC, stn, stn),
        compiler_params=_cparams(("parallel", "arbitrary")),
        name="mlstm_chunks",
    )(q, k, v, o_raw, bt[..., None], rows[0], igt[..., None], rows[1], C0, n0[:, :, None, :], m0b,
      norm_w[None, :])
    return out, C, n[:, :, 0, :], m[:, :, 0, 0]


def _split(u, sizes):
    out, off = [], 0
    for s in sizes:
        out.append(u[..., off:off + s])
        off += s
    return out


def _rmsnorm(x, g):
    xf = x.astype(F32)
    y = xf * lax.rsqrt(jnp.mean(xf * xf, axis=-1, keepdims=True) + EPS)
    return (y * g.astype(F32)).astype(x.dtype)


def _headnorm(y, eps):
    mu = jnp.mean(y, axis=-1, keepdims=True)
    var = jnp.mean(jnp.square(y - mu), axis=-1, keepdims=True)
    return (y - mu) * lax.rsqrt(var + eps)


def _chunk_causal(tq, tk):
    return (tk // CHUNK) <= (tq // CHUNK)


def _frame_causal(tq, tk):
    return tk <= tq


def _rope(x, pos):
    half = MLA_ROPE // 2
    inv = ROPE_THETA ** (-jnp.arange(half, dtype=F32) / half)
    ang = pos.astype(F32)[:, None] * inv[None, :]
    shape = (1, pos.shape[0]) + (1,) * (x.ndim - 3) + (half,)
    cos = jnp.cos(ang).reshape(shape)
    sin = jnp.sin(ang).reshape(shape)
    xf = x.astype(F32)
    x1, x2 = xf[..., :half], xf[..., half:]
    return jnp.concatenate([x1 * cos - x2 * sin, x2 * cos + x1 * sin], axis=-1).astype(x.dtype)


def _head_major(z, tk_pad=None):
    z = jnp.transpose(z, (0, 2, 1, 3)).astype(BF16)
    if tk_pad is not None and tk_pad != z.shape[2]:
        z = jnp.pad(z, ((0, 0), (0, 0), (0, tk_pad - z.shape[2]), (0, 0)))
    return z


def _mla(cq, ckv, kr, past_ckv, past_kr, q_norm, kv_norm, w_uq, w_ukv):
    B, T, _ = cq.shape
    P = past_ckv.shape[1]
    Tk = P + T
    tk_pad = -(-Tk // V7X_LANES) * V7X_LANES
    q_pos = P + jnp.arange(T, dtype=jnp.int32)
    lat_new = _rmsnorm(ckv, kv_norm)
    kr_new = _rope(kr, q_pos)
    q = _mm(_rmsnorm(cq, q_norm).reshape(B * T, MLA_Q_RANK).astype(BF16), w_uq.astype(BF16))
    q = q.reshape(B, T, MLA_HEADS, MLA_NOPE + MLA_ROPE)
    q = jnp.concatenate([q[..., :MLA_NOPE], _rope(q[..., MLA_NOPE:], q_pos)], axis=-1)
    lat = jnp.concatenate([past_ckv, lat_new], axis=1)
    k_rope = jnp.concatenate([past_kr, kr_new], axis=1)
    kv = _mm(lat.reshape(B * Tk, MLA_KV_RANK).astype(BF16), w_ukv.astype(BF16))
    kv = kv.reshape(B, Tk, MLA_HEADS, MLA_NOPE + MLA_V)
    k = jnp.concatenate([kv[..., :MLA_NOPE],
                         jnp.broadcast_to(k_rope[:, :, None, :], (B, Tk, MLA_HEADS, MLA_ROPE))], axis=-1)
    o = _attention(_head_major(q), _head_major(k, tk_pad), _head_major(kv[..., MLA_NOPE:], tk_pad),
                   tk_valid=Tk, chunk_mask=True)
    return jnp.transpose(o, (0, 2, 1, 3)).reshape(B, T, BRANCH_W), lat_new, kr_new


def _rwkv7(u, prev_u, S0, mu, w0, w_lora, a0, a_lora, g_lora, k_k, k_a, r_k, gn_w, gn_b):
    B, T, _ = u.shape
    H, hd = RWKV_HEADS, RWKV_HD
    u_prev = jnp.concatenate([prev_u[:, None, :], u[:, :-1]], axis=1)
    um = u + (u_prev - u) * mu
    r, k, v, wd, ad, gd = _split(um, [BRANCH_W] * 3 + [RWKV_W_LORA, RWKV_A_LORA, RWKV_G_LORA])
    def lora(z, w):
        return _mm(z.reshape(B * T, z.shape[-1]).astype(BF16), w.astype(BF16)).reshape(B, T, w.shape[1])

    w_raw = (w0 + lora(jnp.tanh(wd), w_lora)).astype(F32)
    decay = jnp.exp(-jnp.exp(-jax.nn.softplus(-w_raw) - 0.5))
    a = jax.nn.sigmoid((a0 + lora(ad, a_lora)).astype(F32))
    g = lora(jax.nn.sigmoid(gd), g_lora)
    kk = k * k_k
    k = k.astype(F32) * (1.0 + (a - 1.0) * k_a.astype(F32))

    nl = B * H
    nl_pad = -(-nl // V7X_LANES) * V7X_LANES

    def lanes(z):
        z = jnp.transpose(z.reshape(B, T, H, hd), (1, 3, 0, 2)).reshape(T, hd, nl)
        return jnp.pad(z, ((0, 0), (0, 0), (0, nl_pad - nl)))

    s0 = jnp.transpose(S0.astype(F32), (3, 2, 0, 1)).reshape(hd, hd, nl)
    s0 = jnp.pad(s0, ((0, 0), (0, 0), (0, nl_pad - nl)))
    ys, s_fin = _rwkv_scan(lanes(r), lanes(decay), lanes(k), lanes(v), lanes(kk), lanes(a), s0)
    ys = jnp.transpose(ys[:, :, :nl].reshape(T, hd, B, H), (2, 0, 3, 1))
    s_fin = jnp.transpose(s_fin[:, :, :nl].reshape(hd, hd, B, H), (2, 3, 1, 0))

    def heads(z):
        return z.astype(F32).reshape(B, T, H, hd)

    rh, kh, vh = heads(r), heads(k), heads(v)
    y = _headnorm(ys, RWKV_GN_EPS).reshape(B, T, BRANCH_W) * gn_w.astype(F32) + gn_b.astype(F32)
    bonus = jnp.sum(rh * kh * r_k.astype(F32), axis=-1, keepdims=True) * vh
    out = (y + bonus.reshape(B, T, BRANCH_W)) * g.astype(F32)
    return out.astype(u.dtype), s_fin.astype(u.dtype), u[:, -1]


def _fox(q, k, v, f_raw, past_k, past_v, past_logf, b_f):
    B, T, _ = q.shape
    P = past_k.shape[1]

    def hd(z):
        return z.reshape(B, T, FOX_HEADS, FOX_HD)

    qh, kh, vh = hd(q), hd(k), hd(v)
    logf = jax.nn.log_sigmoid((f_raw + b_f).astype(F32))
    F = jnp.cumsum(jnp.concatenate([past_logf.astype(F32), logf], axis=1), axis=1)
    Tk = P + T
    tk_pad = -(-Tk // V7X_LANES) * V7X_LANES
    K = jnp.concatenate([past_k, kh], axis=1)
    V = jnp.concatenate([past_v, vh], axis=1)
    Ft = jnp.swapaxes(F, 1, 2)
    o = _attention(_head_major(qh), _head_major(K, tk_pad), _head_major(V, tk_pad),
                   tk_valid=Tk, chunk_mask=False,
                   fq=Ft[:, :, P:], fk=jnp.pad(Ft, ((0, 0), (0, 0), (0, tk_pad - Tk))))
    return jnp.transpose(o, (0, 2, 1, 3)).reshape(B, T, BRANCH_W), kh, vh, logf.astype(q.dtype)


def _mlstm(q_raw, k_raw, v, o_raw, i_raw, f_raw, conv_prev, C0, n0, m0, conv_w, conv_b, b_i, b_f, norm_w):
    B, T, _ = q_raw.shape
    xp = jnp.concatenate([conv_prev, jnp.concatenate([q_raw, k_raw], axis=-1)], axis=1)
    conv = conv_b
    for j in range(MLSTM_CONV):
        conv = conv + xp[:, j:j + T] * conv_w[j]
    qk = jax.nn.silu(conv)
    q = qk[..., :BRANCH_W]
    k = qk[..., BRANCH_W:] * (MLSTM_HD ** -0.5)
    ig = (i_raw + b_i).astype(F32)
    lf = jax.nn.log_sigmoid((f_raw + b_f).astype(F32))
    out, C, n, m = _mlstm_chunks(q, k, v, o_raw, ig, lf, C0.astype(F32), n0.astype(F32), m0.astype(F32),
                                 norm_w.astype(F32))
    dt = q_raw.dtype
    return out.astype(dt), xp[:, T:], C.astype(dt), n.astype(dt), m.astype(dt)


def _mixers(x, mod, st, p):
    B, T, D = x.shape
    sh1, sc1, g1, sh2, sc2, g2 = jnp.split(mod[:, None, :], 6, axis=-1)
    h = _rmsnorm(x, p['norm1']) * (1.0 + sc1) + sh1
    hb = h.reshape(B * T, D).astype(BF16)
    widths = (MLA_IN, RWKV_IN, FOX_IN, MLSTM_IN, GATE_IN)
    u_mla, u_rw, u_fox, u_ml, u_gate = [_mm(hb, w)[:, :n].reshape(B, T, n) for w, n in zip(p['w_sec'], widths)]
    cq, ckv, kr = _split(u_mla, [MLA_Q_RANK, MLA_KV_RANK, MLA_ROPE])
    o_mla, lat_new, kr_new = _mla(cq, ckv, kr, st['mla_ckv'], st['mla_krope'], p['mla_q_norm'],
                                  p['mla_kv_norm'], p['mla_w_uq'], p['mla_w_ukv'])
    o_rw, s_new, shift_new = _rwkv7(u_rw, st['rwkv_shift'], st['rwkv_S'], p['rwkv_mu'], p['rwkv_w0'],
                                    p['rwkv_w_lora'], p['rwkv_a0'], p['rwkv_a_lora'], p['rwkv_g_lora'],
                                    p['rwkv_k_k'], p['rwkv_k_a'], p['rwkv_r_k'], p['rwkv_gn_w'], p['rwkv_gn_b'])
    fq, fk, fv, ff = _split(u_fox, [BRANCH_W] * 3 + [FOX_HEADS])
    o_fox, fk_new, fv_new, lf_new = _fox(fq, fk, fv, ff, st['fox_k'], st['fox_v'], st['fox_logf'], p['fox_b_f'])
    mq, mk, mv, mo, mi, mf = _split(u_ml, [BRANCH_W] * 4 + [MLSTM_HEADS] * 2)
    o_ml, conv_new, C_new, n_new, m_new = _mlstm(mq, mk, mv, mo, mi, mf, st['mlstm_conv'], st['mlstm_C'],
                                                 st['mlstm_n'], st['mlstm_m'], p['mlstm_conv_w'],
                                                 p['mlstm_conv_b'], p['mlstm_b_i'], p['mlstm_b_f'],
                                                 p['mlstm_norm_w'])
    o_all = jnp.stack([o.reshape(B * T, BRANCH_W).astype(BF16) for o in (o_mla, o_rw, o_fox, o_ml)])
    mixed = _merge(o_all, p['w_branch_bf'], u_gate.reshape(B * T, GATE_IN))
    x = x + g1 * _mm(mixed, p['w_out_bf']).reshape(B, T, D)
    h2 = _rmsnorm(x, p['norm2']) * (1.0 + sc2) + sh2
    new = {'mla_ckv': lat_new, 'mla_krope': kr_new, 'fox_k': fk_new, 'fox_v': fv_new, 'fox_logf': lf_new,
           'rwkv_S': s_new, 'rwkv_shift': shift_new, 'mlstm_C': C_new, 'mlstm_n': n_new,
           'mlstm_m': m_new, 'mlstm_conv': conv_new}
    return x, h2.reshape(B * T, D).astype(BF16), g2, new


def kernel(x_prompt, x_sample, c_prompt, c_sample,
           cache_mla_ckv, cache_mla_krope, cache_fox_k, cache_fox_v, cache_fox_logf,
           state_rwkv_S, state_rwkv_shift, state_mlstm_C, state_mlstm_n, state_mlstm_m, state_mlstm_conv,
           w_ada, b_ada, norm1, norm2, w_in,
           mla_q_norm, mla_kv_norm, mla_w_uq, mla_w_ukv,
           rwkv_mu, rwkv_w0, rwkv_w_lora, rwkv_a0, rwkv_a_lora, rwkv_g_lora, rwkv_k_k, rwkv_k_a,
           rwkv_r_k, rwkv_gn_w, rwkv_gn_b,
           fox_b_f,
           mlstm_conv_w, mlstm_conv_b, mlstm_b_i, mlstm_b_f, mlstm_norm_w,
           w_branch, w_out, router_w, router_b, expert_w1, expert_b1, expert_w2, expert_b2, final_norm):
    params = dict(w_ada=w_ada, b_ada=b_ada, norm1=norm1, norm2=norm2,
                  mla_q_norm=mla_q_norm, mla_kv_norm=mla_kv_norm, mla_w_uq=mla_w_uq, mla_w_ukv=mla_w_ukv,
                  rwkv_mu=rwkv_mu, rwkv_w0=rwkv_w0, rwkv_w_lora=rwkv_w_lora, rwkv_a0=rwkv_a0,
                  rwkv_a_lora=rwkv_a_lora, rwkv_g_lora=rwkv_g_lora, rwkv_k_k=rwkv_k_k, rwkv_k_a=rwkv_k_a,
                  rwkv_r_k=rwkv_r_k, rwkv_gn_w=rwkv_gn_w, rwkv_gn_b=rwkv_gn_b, fox_b_f=fox_b_f,
                  mlstm_conv_w=mlstm_conv_w, mlstm_conv_b=mlstm_conv_b, mlstm_b_i=mlstm_b_i,
                  mlstm_b_f=mlstm_b_f, mlstm_norm_w=mlstm_norm_w,
                  router_w=router_w, router_b=router_b)
    Bp, Tp, D = x_prompt.shape
    Bs, Ts, _ = x_sample.shape
    dt = x_prompt.dtype
    c_all = jnp.concatenate([c_prompt, c_sample], axis=0)
    c_all = jnp.pad(jax.nn.silu(c_all), ((0, -c_all.shape[0] % 16), (0, 0))).astype(BF16)

    prompt_past = dict(
        mla_ckv=jnp.zeros((DEPTH, Bp, 0, MLA_KV_RANK), dt),
        mla_krope=jnp.zeros((DEPTH, Bp, 0, MLA_ROPE), dt),
        fox_k=jnp.zeros((DEPTH, Bp, 0, FOX_HEADS, FOX_HD), dt),
        fox_v=jnp.zeros((DEPTH, Bp, 0, FOX_HEADS, FOX_HD), dt),
        fox_logf=jnp.zeros((DEPTH, Bp, 0, FOX_HEADS), dt),
        rwkv_S=jnp.zeros((DEPTH, Bp, RWKV_HEADS, RWKV_HD, RWKV_HD), dt),
        rwkv_shift=jnp.zeros((DEPTH, Bp, RWKV_IN), dt),
        mlstm_C=jnp.zeros((DEPTH, Bp, MLSTM_HEADS, MLSTM_HD, MLSTM_HD), dt),
        mlstm_n=jnp.zeros((DEPTH, Bp, MLSTM_HEADS, MLSTM_HD), dt),
        mlstm_m=jnp.zeros((DEPTH, Bp, MLSTM_HEADS), dt),
        mlstm_conv=jnp.zeros((DEPTH, Bp, MLSTM_CONV - 1, 2 * BRANCH_W), dt))
    sample_past = dict(
        mla_ckv=cache_mla_ckv, mla_krope=cache_mla_krope, fox_k=cache_fox_k, fox_v=cache_fox_v,
        fox_logf=cache_fox_logf, rwkv_S=state_rwkv_S, rwkv_shift=state_rwkv_shift,
        mlstm_C=state_mlstm_C, mlstm_n=state_mlstm_n, mlstm_m=state_mlstm_m, mlstm_conv=state_mlstm_conv)

    xs = [x_prompt, x_sample]
    cs = [c_prompt, c_sample]
    pasts = [prompt_past, sample_past]
    news = [{}, {}]
    for l in range(DEPTH):
        p = {name: arr[l] for name, arr in params.items()}
        wl = w_in[l]
        p['w_sec'], off = [], 0
        for n in (MLA_IN, RWKV_IN, FOX_IN, MLSTM_IN, GATE_IN):
            p['w_sec'].append(jnp.pad(wl[:, off:off + n], ((0, 0), (0, -n % 512))).astype(BF16))
            off += n
        p['w_branch_bf'] = w_branch[l].astype(BF16)
        p['w_out_bf'] = w_out[l].astype(BF16)
        mod_all = _mm(c_all, p['w_ada'].astype(BF16)) + p['b_ada']
        mods = [mod_all[:Bp], mod_all[Bp:Bp + Bs]]
        mids, h2s, g2s = [], [], []
        for gi in range(2):
            mod = mods[gi]
            st = {name: arr[l] for name, arr in pasts[gi].items()}
            x_mid, h2, g2, ns = _mixers(xs[gi], mod, st, p)
            mids.append(x_mid)
            h2s.append(h2)
            g2s.append(g2)
            for name, arr in ns.items():
                news[gi].setdefault(name, []).append(arr)
        moe_out = _moe(jnp.concatenate(h2s, axis=0), p['router_w'], p['router_b'],
                       expert_w1, expert_b1, expert_w2, expert_b2, l)
        n0 = Bp * Tp
        xs[0] = mids[0] + g2s[0] * moe_out[:n0].reshape(Bp, Tp, D)
        xs[1] = mids[1] + g2s[1] * moe_out[n0:].reshape(Bs, Ts, D)
    y_prompt = _rmsnorm(xs[0], final_norm)
    y_sample = _rmsnorm(xs[1], final_norm)
    sp = {name: jnp.stack(arrs) for name, arrs in news[0].items()}
    ss = {name: jnp.stack(arrs) for name, arrs in news[1].items()}
    return (y_prompt, y_sample,
            sp['mla_ckv'], ss['mla_ckv'], sp['mla_krope'], ss['mla_krope'],
            sp['fox_k'], ss['fox_k'], sp['fox_v'], ss['fox_v'], sp['fox_logf'], ss['fox_logf'],
            sp['rwkv_S'], ss['rwkv_S'], sp['rwkv_shift'], ss['rwkv_shift'],
            sp['mlstm_C'], ss['mlstm_C'], sp['mlstm_n'], ss['mlstm_n'], sp['mlstm_m'], ss['mlstm_m'],
            sp['mlstm_conv'], ss['mlstm_conv'])
```

```python
import functools

import jax
import jax.numpy as jnp
from jax import lax
from jax.experimental import pallas as pl
from jax.experimental.pallas import tpu as pltpu

D_MODEL = 4096
DEPTH = 2
CHUNK = 64
Q_BLOCK = 128
N_BRANCH = 4
BRANCH_W = D_MODEL // N_BRANCH
MLA_HEADS = 8
MLA_NOPE = 128
MLA_ROPE = 64
MLA_V = BRANCH_W // MLA_HEADS
MLA_Q_RANK = 768
MLA_KV_RANK = 512
ROPE_THETA = 10000.0
RWKV_HD = 64
RWKV_HEADS = BRANCH_W // RWKV_HD
RWKV_W_LORA = 64
RWKV_A_LORA = 64
RWKV_G_LORA = 128
RWKV_GN_EPS = 64e-5
RWKV_IN = 3 * BRANCH_W + RWKV_W_LORA + RWKV_A_LORA + RWKV_G_LORA
FOX_HEADS = 8
FOX_HD = BRANCH_W // FOX_HEADS
MLSTM_HEADS = 8
MLSTM_HD = BRANCH_W // MLSTM_HEADS
MLSTM_CONV = 4
N_EXPERTS = 32
TOP_K = 4
EXPERT_FF = D_MODEL // 2
SWIGLU_LIMIT = 7.0
SWIGLU_ALPHA = 1.702
EPS = 1e-6
NEG_INF = -1e30
MLA_IN = MLA_Q_RANK + MLA_KV_RANK + MLA_ROPE
FOX_IN = 3 * BRANCH_W + FOX_HEADS
MLSTM_IN = 4 * BRANCH_W + 2 * MLSTM_HEADS
GATE_IN = N_BRANCH * D_MODEL
IN_W = MLA_IN + RWKV_IN + FOX_IN + MLSTM_IN + GATE_IN

V7X_LANES = 128
V7X_VMEM_LIMIT_BYTES = 56 * 1024 * 1024
BF16 = jnp.bfloat16
F32 = jnp.float32


def _tile(n, pref):
    t = pref
    while t >= 8:
        if n % t == 0:
            return t
        t //= 2
    return n


def _cparams(sem):
    return pltpu.CompilerParams(dimension_semantics=sem, vmem_limit_bytes=V7X_VMEM_LIMIT_BYTES)


def _mm_kernel(a_ref, w_ref, o_ref):
    o_ref[...] = jnp.dot(a_ref[...], w_ref[...].astype(BF16), preferred_element_type=F32)


def _mm(a, w, layer=None, tm_pref=1024, tn_pref=512):
    M, K = a.shape
    N = w.shape[-1]
    tm, tn = _tile(M, tm_pref), _tile(N, tn_pref)
    if layer is None:
        w_spec = pl.BlockSpec((K, tn), lambda i, j: (0, j))
    else:
        w_spec = pl.BlockSpec((None, K, tn), lambda i, j: (layer, 0, j))
    return pl.pallas_call(
        _mm_kernel,
        out_shape=jax.ShapeDtypeStruct((M, N), F32),
        grid=(M // tm, N // tn),
        in_specs=[pl.BlockSpec((tm, K), lambda i, j: (i, 0)), w_spec],
        out_specs=pl.BlockSpec((tm, tn), lambda i, j: (i, j)),
        compiler_params=_cparams(("parallel", "parallel")),
        name="dense_mm",
    )(a, w)


def _merge_kernel(o_ref, wb_ref, ug_ref, out_ref, acc_ref):
    b = pl.program_id(2)

    @pl.when(b == 0)
    def _():
        acc_ref[...] = jnp.zeros_like(acc_ref)

    acc_ref[...] += jax.nn.sigmoid(ug_ref[...]) * jnp.dot(o_ref[0], wb_ref[0], preferred_element_type=F32)

    @pl.when(b == pl.num_programs(2) - 1)
    def _():
        out_ref[...] = acc_ref[...].astype(out_ref.dtype)


def _merge(o_all, wb, u, tm_pref=1024, tn_pref=1024):
    nb, M, W = o_all.shape
    D = wb.shape[2]
    tm, tn = _tile(M, tm_pref), _tile(D, tn_pref)
    npb = D // tn
    return pl.pallas_call(
        _merge_kernel,
        out_shape=jax.ShapeDtypeStruct((M, D), BF16),
        grid=(M // tm, npb, nb),
        in_specs=[pl.BlockSpec((1, tm, W), lambda i, j, b: (b, i, 0)),
                  pl.BlockSpec((1, W, tn), lambda i, j, b: (b, 0, j)),
                  pl.BlockSpec((tm, tn), lambda i, j, b: (i, b * npb + j))],
        out_specs=pl.BlockSpec((tm, tn), lambda i, j, b: (i, j)),
        scratch_shapes=[pltpu.VMEM((tm, tn), F32)],
        compiler_params=_cparams(("parallel", "parallel", "arbitrary")),
        name="branch_merge",
    )(o_all, wb, u)


def _rwkv_scan_kernel(r_ref, w_ref, k_ref, v_ref, kk_ref, a_ref, s0_ref, y_ref, s_ref, kkn_scr, b_scr):
    hd = s_ref.shape[0]
    steps = r_ref.shape[0]
    n_acc = 4

    @pl.when(pl.program_id(1) == 0)
    def _():
        s_ref[...] = s0_ref[...]

    def step(t, carry):
        kkraw = kk_ref[t]
        nrm = jnp.sqrt(jnp.sum(kkraw * kkraw, axis=0, keepdims=True))
        kkn = kkraw / jnp.maximum(nrm, 1e-12)
        kkn_scr[...] = kkn
        b_scr[...] = kkn * a_ref[t]
        acc = [None] * n_acc
        for j in range(hd):
            term = s_ref[j] * kkn_scr[pl.ds(j, 1), :]
            acc[j % n_acc] = term if acc[j % n_acc] is None else acc[j % n_acc] + term
        sa = -functools.reduce(lambda x, y: x + y, acc)
        v = v_ref[t]
        yacc = [None] * n_acc
        for j in range(hd):
            sj = (s_ref[j] * w_ref[t, pl.ds(j, 1), :] + sa * b_scr[pl.ds(j, 1), :]
                  + v * k_ref[t, pl.ds(j, 1), :])
            s_ref[j] = sj
            term = sj * r_ref[t, pl.ds(j, 1), :]
            yacc[j % n_acc] = term if yacc[j % n_acc] is None else yacc[j % n_acc] + term
        y_ref[t] = functools.reduce(lambda x, y: x + y, yacc)
        return carry

    lax.fori_loop(0, steps, step, 0)


def _rwkv_scan(r, w, k, v, kk, a, s0, tb_pref=32):
    T, hd, NL = r.shape
    tb = _tile(T, tb_pref)
    seq = pl.BlockSpec((tb, hd, V7X_LANES), lambda g, t: (t, 0, g))
    st = pl.BlockSpec((hd, hd, V7X_LANES), lambda g, t: (0, 0, g))
    return pl.pallas_call(
        _rwkv_scan_kernel,
        out_shape=(jax.ShapeDtypeStruct((T, hd, NL), F32), jax.ShapeDtypeStruct((hd, hd, NL), F32)),
        grid=(NL // V7X_LANES, T // tb),
        in_specs=[seq] * 6 + [st],
        out_specs=(seq, st),
        scratch_shapes=[pltpu.VMEM((hd, V7X_LANES), F32), pltpu.VMEM((hd, V7X_LANES), F32)],
        compiler_params=_cparams(("parallel", "arbitrary")),
        name="rwkv_scan",
    )(r, w, k, v, kk, a, s0)


def _expert_kernel(te_ref, tv_ref, x_ref, w1g_ref, w1u_ref, b1g_ref, b1u_ref, w2_ref, b2_ref, rw_ref, o_ref,
                   act_scr):
    i, s = pl.program_id(0), pl.program_id(1)
    nf, tm, tf = act_scr.shape
    th = tm // 2
    valid, second = tv_ref[i] >= 1, tv_ref[i] == 2

    @pl.when(jnp.logical_and(valid, s < nf))
    def _():
        wg, wu = w1g_ref[0].astype(BF16), w1u_ref[0].astype(BF16)

        def half(lo):
            x = x_ref[lo:lo + th, :]
            g = jnp.dot(x, wg, preferred_element_type=F32) + b1g_ref[0]
            u = jnp.dot(x, wu, preferred_element_type=F32) + b1u_ref[0]
            g = jnp.minimum(g, SWIGLU_LIMIT)
            u = jnp.clip(u, -SWIGLU_LIMIT, SWIGLU_LIMIT)
            act_scr[s, lo:lo + th, :] = ((u + 1.0) * g * jax.nn.sigmoid(SWIGLU_ALPHA * g)).astype(BF16)

        half(0)
        pl.when(second)(lambda: half(th))

    @pl.when(jnp.logical_and(valid, s >= nf))
    def _():
        w2 = [w2_ref[0, j * tf:(j + 1) * tf, :].astype(BF16) for j in range(nf)]

        def half(lo):
            acc = None
            for j in range(nf):
                part = jnp.dot(act_scr[j, lo:lo + th, :], w2[j], preferred_element_type=F32)
                acc = part if acc is None else acc + part
            o_ref[lo:lo + th, :] = (acc + b2_ref[0]) * rw_ref[lo:lo + th, :]

        half(0)
        pl.when(second)(lambda: half(th))

    @pl.when(jnp.logical_and(jnp.logical_not(valid), s >= nf))
    def _():
        o_ref[:th, :] = jnp.zeros((th, o_ref.shape[1]), o_ref.dtype)

    @pl.when(jnp.logical_and(jnp.logical_not(second), s >= nf))
    def _():
        o_ref[th:, :] = jnp.zeros((th, o_ref.shape[1]), o_ref.dtype)


def _experts(xg, w1, b1, w2, b2, roww, tile_e, tile_v, tm, tf_pref=256, tn_pref=512):
    R, D = xg.shape
    F = w2.shape[1]
    tf, tn = _tile(F, tf_pref), _tile(D, tn_pref)
    nf, nn = F // tf, D // tn

    def c1(i, s, tv):
        return jnp.where(tv[i] >= 1, jnp.minimum(s, nf - 1), nf - 1)

    def c2(i, s, tv):
        return jnp.where(tv[i] >= 1, jnp.maximum(s - nf, 0), nn - 1)

    grid_spec = pltpu.PrefetchScalarGridSpec(
        num_scalar_prefetch=2,
        grid=(R // tm, nf + nn),
        in_specs=[
            pl.BlockSpec((tm, D), lambda i, s, te, tv: (i, 0), pipeline_mode=pl.Buffered(1)),
            pl.BlockSpec((1, D, tf), lambda i, s, te, tv: (te[i], 0, c1(i, s, tv))),
            pl.BlockSpec((1, D, tf), lambda i, s, te, tv: (te[i], 0, nf + c1(i, s, tv))),
            pl.BlockSpec((1, 1, tf), lambda i, s, te, tv: (te[i], 0, c1(i, s, tv))),
            pl.BlockSpec((1, 1, tf), lambda i, s, te, tv: (te[i], 0, nf + c1(i, s, tv))),
            pl.BlockSpec((1, F, tn), lambda i, s, te, tv: (te[i], 0, c2(i, s, tv))),
            pl.BlockSpec((1, 1, tn), lambda i, s, te, tv: (te[i], 0, c2(i, s, tv))),
            pl.BlockSpec((tm, 1), lambda i, s, te, tv: (i, 0)),
        ],
        out_specs=pl.BlockSpec((tm, tn), lambda i, s, te, tv: (i, jnp.maximum(s - nf, 0))),
        scratch_shapes=[pltpu.VMEM((nf, tm, tf), BF16)],
    )
    return pl.pallas_call(
        _expert_kernel,
        out_shape=jax.ShapeDtypeStruct((R, D), F32),
        grid_spec=grid_spec,
        compiler_params=_cparams(("parallel", "arbitrary")),
        name="moe_experts",
    )(tile_e, tile_v, xg, w1, w1, b1, b1, w2, b2, roww)


def _moe(x, router_w, router_b, w1, b1, w2, b2, layer, tm_pref=1024):
    N, D = x.shape
    E = router_w.shape[1]
    w1 = w1.reshape((-1,) + w1.shape[2:])
    w2 = w2.reshape((-1,) + w2.shape[2:])
    b1 = b1.reshape(-1, 1, b1.shape[-1])
    b2 = b2.reshape(-1, 1, b2.shape[-1])
    logits = jnp.dot(x.astype(F32), router_w, precision=lax.Precision.HIGHEST) + router_b
    top_v, top_i = lax.top_k(logits, TOP_K)
    top_w = jax.nn.softmax(top_v, axis=-1)
    npair = N * TOP_K
    tm = _tile(npair, tm_pref)
    R = npair + E * tm
    e_flat = top_i.reshape(-1).astype(jnp.int32)
    onehot = (e_flat[:, None] == jnp.arange(E, dtype=jnp.int32)[None, :]).astype(jnp.int32)
    csum = jnp.cumsum(onehot, axis=0)
    counts = csum[-1]
    rank = jnp.sum((csum - onehot) * onehot, axis=1)
    starts = jnp.cumsum(counts) - counts
    padded = ((counts + tm - 1) // tm) * tm
    pend = jnp.cumsum(padded)
    pstart = pend - padded
    pos = pstart[e_flat] + rank
    order = jnp.argsort(e_flat, stable=True).astype(jnp.int32)
    rows = jnp.arange(R, dtype=jnp.int32)
    row_e = jnp.minimum(jnp.sum((rows[:, None] >= pend[None, :]).astype(jnp.int32), axis=1), E - 1)
    row_rank = rows - pstart[row_e]
    row_ok = jnp.logical_and(rows < pend[-1], row_rank < counts[row_e])
    row_pair = order[jnp.clip(starts[row_e] + row_rank, 0, npair - 1)]
    src_tok = jnp.where(row_ok, row_pair // TOP_K, rows % N)
    roww = jnp.where(row_ok, top_w.reshape(-1)[row_pair], 0.0)
    tile_e = row_e[::tm]
    tile_v = jnp.sum(row_ok[::tm // 2].astype(jnp.int32).reshape(-1, 2), axis=1)
    last_e = tile_e[jnp.maximum(jnp.sum((tile_v >= 1).astype(jnp.int32)) - 1, 0)]
    tile_e = jnp.where(tile_v >= 1, tile_e, last_e) + layer * E
    xg = jnp.take(x, src_tok, axis=0, mode="clip")
    y = _experts(xg, w1, b1, w2, b2, roww[:, None], tile_e, tile_v, tm)
    pos = pos.reshape(N, TOP_K)
    parts = [jnp.take(y, pos[:, s], axis=0, mode="clip") for s in range(TOP_K)]
    return functools.reduce(lambda a, b: a + b, parts)


def _attn_kernel(*refs, scale, tk, q_off, tk_valid, chunk_mask, use_bias):
    if use_bias:
        q_ref, k_ref, v_ref, fq_ref, fk_ref, o_ref, m_scr, l_scr, acc_scr = refs
    else:
        q_ref, k_ref, v_ref, o_ref, m_scr, l_scr, acc_scr = refs
    tq = q_ref.shape[2]
    n_kb_total = k_ref.shape[2] // tk
    qi = pl.program_id(2)
    q = q_ref[0, 0]
    m_scr[...] = jnp.full_like(m_scr, NEG_INF)
    l_scr[...] = jnp.zeros_like(l_scr)
    acc_scr[...] = jnp.zeros_like(acc_scr)
    q_pos = q_off + qi * tq + lax.broadcasted_iota(jnp.int32, (tq, tk), 0)
    last_q = q_off + (qi + 1) * tq - 1
    limit = (last_q // CHUNK + 1) * CHUNK if chunk_mask else last_q + 1
    n_kb = jnp.minimum((limit + tk - 1) // tk, n_kb_total)
    first_q = q_off + qi * tq
    all_visible = (first_q // CHUNK + 1) * CHUNK if chunk_mask else first_q + 1
    n_full = jnp.minimum(jnp.minimum(all_visible, tk_valid) // tk, n_kb)

    def body(kb, carry, masked):
        start = pl.multiple_of(kb * tk, tk)
        k = k_ref[0, 0, pl.ds(start, tk), :]
        v = v_ref[0, 0, pl.ds(start, tk), :]
        s = lax.dot_general(q, k, (((1,), (1,)), ((), ())), preferred_element_type=F32) * scale
        if use_bias:
            s = s + (fq_ref[0, 0] - fk_ref[0, 0, kb])
        if masked:
            k_pos = kb * tk + lax.broadcasted_iota(jnp.int32, (tq, tk), 1)
            ok = (k_pos // CHUNK <= q_pos // CHUNK) if chunk_mask else (k_pos <= q_pos)
            ok = jnp.logical_and(ok, k_pos < tk_valid)
            s = jnp.where(ok, s, NEG_INF)
        m_old = m_scr[...]
        m_new = jnp.maximum(m_old, jnp.max(s, axis=1, keepdims=True))
        alpha = jnp.exp(m_old - m_new)
        p = jnp.exp(s - m_new)
        l_scr[...] = alpha * l_scr[...] + jnp.sum(p, axis=1, keepdims=True)
        acc_scr[...] = alpha * acc_scr[...] + jnp.dot(p.astype(v.dtype), v, preferred_element_type=F32)
        m_scr[...] = m_new
        return carry

    lax.fori_loop(0, n_full, functools.partial(body, masked=False), 0)
    lax.fori_loop(n_full, n_kb, functools.partial(body, masked=True), 0)
    o_ref[0, 0] = acc_scr[...] / l_scr[...]


def _attention(q, k, v, *, tk_valid, chunk_mask, fq=None, fk=None, tq_pref=512):
    B, H, Tq, dq = q.shape
    Tk, dv = k.shape[2], v.shape[3]
    tq = _tile(Tq, tq_pref)
    tk = Tk if tq * Tk * 4 <= 2 * 1024 * 1024 else _tile(Tk, 512)
    use_bias = fq is not None
    in_specs = [pl.BlockSpec((1, 1, tq, dq), lambda b, h, i: (b, h, i, 0)),
                pl.BlockSpec((1, 1, Tk, dq), lambda b, h, i: (b, h, 0, 0)),
                pl.BlockSpec((1, 1, Tk, dv), lambda b, h, i: (b, h, 0, 0))]
    args = [q, k, v]
    if use_bias:
        in_specs += [pl.BlockSpec((1, 1, tq, 1), lambda b, h, i: (b, h, i, 0)),
                     pl.BlockSpec((1, 1, Tk // tk, 1, tk), lambda b, h, i: (b, h, 0, 0, 0))]
        args += [fq[..., None], fk.reshape(B, H, Tk // tk, 1, tk)]
    return pl.pallas_call(
        functools.partial(_attn_kernel, scale=dq ** -0.5, tk=tk, q_off=tk_valid - Tq, tk_valid=tk_valid,
                          chunk_mask=chunk_mask, use_bias=use_bias),
        out_shape=jax.ShapeDtypeStruct((B, H, Tq, dv), F32),
        grid=(B, H, Tq // tq),
        in_specs=in_specs,
        out_specs=pl.BlockSpec((1, 1, tq, dv), lambda b, h, i: (b, h, i, 0)),
        scratch_shapes=[pltpu.VMEM((tq, 1), F32), pltpu.VMEM((tq, 1), F32), pltpu.VMEM((tq, dv), F32)],
        compiler_params=_cparams(("parallel", "parallel", "arbitrary")),
        name="causal_attention",
    )(*args)


def _mlstm_kernel(q_ref, k_ref, v_ref, o_ref, bcol_ref, brow_ref, igcol_ref, igrow_ref, c0_ref, n0_ref, m0_ref,
                  nw_ref, out_ref, c_ref, n_ref, m_ref):
    H, d = c_ref.shape[1], c_ref.shape[2]
    L = q_ref.shape[1]

    @pl.when(pl.program_id(1) == 0)
    def _():
        c_ref[...] = c0_ref[...]
        n_ref[...] = n0_ref[...]
        m_ref[...] = m0_ref[...]

    tri = lax.broadcasted_iota(jnp.int32, (L, L), 1) <= lax.broadcasted_iota(jnp.int32, (L, L), 0)
    nt = (((1,), (1,)), ((), ()))
    for h in range(H):
        sl = slice(h * d, (h + 1) * d)
        q, k, v = q_ref[0, :, sl], k_ref[0, :, sl], v_ref[0, :, sl]
        bcol, brow = bcol_ref[0, h], brow_ref[0, h, 0]
        igcol, igrow = igcol_ref[0, h], igrow_ref[0, h, 0]
        mp = m_ref[0, h][:, :1]
        dlog = jnp.where(tri, bcol - brow + igrow, NEG_INF)
        inter = bcol + mp
        m_t = jnp.maximum(inter, jnp.max(dlog, axis=1, keepdims=True))
        dw = jnp.exp(dlog - m_t)
        iw = jnp.exp(inter - m_t)
        qb, kb, vb = q.astype(BF16), k.astype(BF16), v.astype(BF16)
        sw = lax.dot_general(qb, kb, nt, preferred_element_type=F32) * dw
        C, n = c_ref[0, h], n_ref[0, h]
        num = (jnp.dot(sw.astype(BF16), vb, preferred_element_type=F32)
               + iw * lax.dot_general(qb, C.astype(BF16), nt, preferred_element_type=F32))
        den = jnp.sum(sw, axis=1, keepdims=True) + iw * jnp.sum(q * n, axis=1, keepdims=True)
        hh = num / jnp.maximum(jnp.abs(den), jnp.exp(-m_t))
        m_new = m_t[L - 1:L, :]
        b_last = bcol[L - 1:L, :]
        wscol = jnp.exp(b_last - bcol + igcol - m_new)
        cw = jnp.exp(b_last + mp - m_new)
        wv_t = jnp.transpose(wscol * v).astype(BF16)
        c_ref[0, h] = cw * C + jnp.dot(wv_t, kb, preferred_element_type=F32)
        n_ref[0, h] = cw * n + jnp.sum(wscol * k, axis=0, keepdims=True)
        m_ref[0, h] = jnp.broadcast_to(m_new, (1, d))
        mu = jnp.mean(hh, axis=1, keepdims=True)
        var = jnp.mean(jnp.square(hh - mu), axis=1, keepdims=True)
        y = (hh - mu) * lax.rsqrt(var + EPS) * nw_ref[:, sl]
        out_ref[0, :, sl] = jax.nn.sigmoid(o_ref[0, :, sl]) * y


def _mlstm_chunks(q, k, v, o_raw, ig, lf, C0, n0, m0, norm_w):
    B, T, W = q.shape
    H = ig.shape[2]
    d = W // H
    L = CHUNK if T % CHUNK == 0 else T
    nc = T // L
    b = jnp.cumsum(lf.reshape(B, nc, L, H), axis=2).reshape(B, T, H)
    bt, igt = jnp.swapaxes(b, 1, 2), jnp.swapaxes(ig, 1, 2)
    seq = pl.BlockSpec((1, L, W), lambda i, c: (i, c, 0))
    col = pl.BlockSpec((1, H, L, 1), lambda i, c: (i, 0, c, 0))
    row = pl.BlockSpec((1, H, 1, 1, L), lambda i, c: (i, 0, c, 0, 0))
    rows = [bt.reshape(B, H, nc, 1, L), igt.reshape(B, H, nc, 1, L)]
    stC = pl.BlockSpec((1, H, d, d), lambda i, c: (i, 0, 0, 0))
    stn = pl.BlockSpec((1, H, 1, d), lambda i, c: (i, 0, 0, 0))
    m0b = jnp.broadcast_to(m0[:, :, None, None], (B, H, 1, d))
    out, C, n, m = pl.pallas_call(
        _mlstm_kernel,
        out_shape=(jax.ShapeDtypeStruct((B, T, W), F32), jax.ShapeDtypeStruct((B, H, d, d), F32),
                   jax.ShapeDtypeStruct((B, H, 1, d), F32), jax.ShapeDtypeStruct((B, H, 1, d), F32)),
        grid=(B, nc),
        in_specs=[seq, seq, seq, seq, col, row, col, row, stC, stn, stn,
                  pl.BlockSpec((1, W), lambda i, c: (0, 0))],
        out_specs=(seq, stC, stn, stn),
        compiler_params=_cparams(("parallel", "arbitrary")),
        name="mlstm_chunks",
    )(q, k, v, o_raw, bt[..., None], rows[0], igt[..., None], rows[1], C0, n0[:, :, None, :], m0b,
      norm_w[None, :])
    return out, C, n[:, :, 0, :], m[:, :, 0, 0]


def _split(u, sizes):
    out, off = [], 0
    for s in sizes:
        out.append(u[..., off:off + s])
        off += s
    return out


def _rmsnorm(x, g):
    xf = x.astype(F32)
    y = xf * lax.rsqrt(jnp.mean(xf * xf, axis=-1, keepdims=True) + EPS)
    return (y * g.astype(F32)).astype(x.dtype)


def _headnorm(y, eps):
    mu = jnp.mean(y, axis=-1, keepdims=True)
    var = jnp.mean(jnp.square(y - mu), axis=-1, keepdims=True)
    return (y - mu) * lax.rsqrt(var + eps)


def _chunk_causal(tq, tk):
    return (tk // CHUNK) <= (tq // CHUNK)


def _frame_causal(tq, tk):
    return tk <= tq


def _rope(x, pos):
    half = MLA_ROPE // 2
    inv = ROPE_THETA ** (-jnp.arange(half, dtype=F32) / half)
    ang = pos.astype(F32)[:, None] * inv[None, :]
    shape = (1, pos.shape[0]) + (1,) * (x.ndim - 3) + (half,)
    cos = jnp.cos(ang).reshape(shape)
    sin = jnp.sin(ang).reshape(shape)
    xf = x.astype(F32)
    x1, x2 = xf[..., :half], xf[..., half:]
    return jnp.concatenate([x1 * cos - x2 * sin, x2 * cos + x1 * sin], axis=-1).astype(x.dtype)


def _head_major(z, tk_pad=None):
    z = jnp.transpose(z, (0, 2, 1, 3)).astype(BF16)
    if tk_pad is not None and tk_pad != z.shape[2]:
        z = jnp.pad(z, ((0, 0), (0, 0), (0, tk_pad - z.shape[2]), (0, 0)))
    return z


def _mla(cq, ckv, kr, past_ckv, past_kr, q_norm, kv_norm, w_uq, w_ukv):
    B, T, _ = cq.shape
    P = past_ckv.shape[1]
    Tk = P + T
    tk_pad = -(-Tk // V7X_LANES) * V7X_LANES
    q_pos = P + jnp.arange(T, dtype=jnp.int32)
    lat_new = _rmsnorm(ckv, kv_norm)
    kr_new = _rope(kr, q_pos)
    q = _mm(_rmsnorm(cq, q_norm).reshape(B * T, MLA_Q_RANK).astype(BF16), w_uq.astype(BF16))
    q = q.reshape(B, T, MLA_HEADS, MLA_NOPE + MLA_ROPE)
    q = jnp.concatenate([q[..., :MLA_NOPE], _rope(q[..., MLA_NOPE:], q_pos)], axis=-1)
    lat = jnp.concatenate([past_ckv, lat_new], axis=1)
    k_rope = jnp.concatenate([past_kr, kr_new], axis=1)
    kv = _mm(lat.reshape(B * Tk, MLA_KV_RANK).astype(BF16), w_ukv.astype(BF16))
    kv = kv.reshape(B, Tk, MLA_HEADS, MLA_NOPE + MLA_V)
    k = jnp.concatenate([kv[..., :MLA_NOPE],
                         jnp.broadcast_to(k_rope[:, :, None, :], (B, Tk, MLA_HEADS, MLA_ROPE))], axis=-1)
    o = _attention(_head_major(q), _head_major(k, tk_pad), _head_major(kv[..., MLA_NOPE:], tk_pad),
                   tk_valid=Tk, chunk_mask=True)
    return jnp.transpose(o, (0, 2, 1, 3)).reshape(B, T, BRANCH_W), lat_new, kr_new


def _rwkv7(u, prev_u, S0, mu, w0, w_lora, a0, a_lora, g_lora, k_k, k_a, r_k, gn_w, gn_b):
    B, T, _ = u.shape
    H, hd = RWKV_HEADS, RWKV_HD
    pad = u.shape[-1] - RWKV_IN
    prev_u = jnp.pad(prev_u, ((0, 0), (0, pad)))
    u_prev = jnp.concatenate([prev_u[:, None, :], u[:, :-1]], axis=1)
    um = u + (u_prev - u) * jnp.pad(mu, (0, pad))
    r, k, v, wd, ad, gd = _split(um, [BRANCH_W] * 3 + [RWKV_W_LORA, RWKV_A_LORA, RWKV_G_LORA])
    def lora(z, w):
        return _mm(z.reshape(B * T, z.shape[-1]).astype(BF16), w.astype(BF16)).reshape(B, T, w.shape[1])

    w_raw = (w0 + lora(jnp.tanh(wd), w_lora)).astype(F32)
    decay = jnp.exp(-jnp.exp(-jax.nn.softplus(-w_raw) - 0.5))
    a = jax.nn.sigmoid((a0 + lora(ad, a_lora)).astype(F32))
    g = lora(jax.nn.sigmoid(gd), g_lora)
    kk = k * k_k
    k = k.astype(F32) * (1.0 + (a - 1.0) * k_a.astype(F32))

    nl = B * H
    nl_pad = -(-nl // V7X_LANES) * V7X_LANES

    def lanes(z):
        z = jnp.transpose(z.reshape(B, T, H, hd), (1, 3, 0, 2)).reshape(T, hd, nl)
        return jnp.pad(z, ((0, 0), (0, 0), (0, nl_pad - nl)))

    s0 = jnp.transpose(S0.astype(F32), (3, 2, 0, 1)).reshape(hd, hd, nl)
    s0 = jnp.pad(s0, ((0, 0), (0, 0), (0, nl_pad - nl)))
    ys, s_fin = _rwkv_scan(lanes(r), lanes(decay), lanes(k), lanes(v), lanes(kk), lanes(a), s0)
    ys = jnp.transpose(ys[:, :, :nl].reshape(T, hd, B, H), (2, 0, 3, 1))
    s_fin = jnp.transpose(s_fin[:, :, :nl].reshape(hd, hd, B, H), (2, 3, 1, 0))

    def heads(z):
        return z.astype(F32).reshape(B, T, H, hd)

    rh, kh, vh = heads(r), heads(k), heads(v)
    y = _headnorm(ys, RWKV_GN_EPS).reshape(B, T, BRANCH_W) * gn_w.astype(F32) + gn_b.astype(F32)
    bonus = jnp.sum(rh * kh * r_k.astype(F32), axis=-1, keepdims=True) * vh
    out = (y + bonus.reshape(B, T, BRANCH_W)) * g.astype(F32)
    return out.astype(u.dtype), s_fin.astype(u.dtype), u[:, -1, :RWKV_IN]


def _fox(q, k, v, f_raw, past_k, past_v, past_logf, b_f):
    B, T, _ = q.shape
    P = past_k.shape[1]

    def hd(z):
        return z.reshape(B, T, FOX_HEADS, FOX_HD)

    qh, kh, vh = hd(q), hd(k), hd(v)
    logf = jax.nn.log_sigmoid((f_raw + b_f).astype(F32))
    F = jnp.cumsum(jnp.concatenate([past_logf.astype(F32), logf], axis=1), axis=1)
    Tk = P + T
    tk_pad = -(-Tk // V7X_LANES) * V7X_LANES
    K = jnp.concatenate([past_k, kh], axis=1)
    V = jnp.concatenate([past_v, vh], axis=1)
    Ft = jnp.swapaxes(F, 1, 2)
    o = _attention(_head_major(qh), _head_major(K, tk_pad), _head_major(V, tk_pad),
                   tk_valid=Tk, chunk_mask=False,
                   fq=Ft[:, :, P:], fk=jnp.pad(Ft, ((0, 0), (0, 0), (0, tk_pad - Tk))))
    return jnp.transpose(o, (0, 2, 1, 3)).reshape(B, T, BRANCH_W), kh, vh, logf.astype(q.dtype)


def _mlstm(q_raw, k_raw, v, o_raw, i_raw, f_raw, conv_prev, C0, n0, m0, conv_w, conv_b, b_i, b_f, norm_w):
    B, T, _ = q_raw.shape
    xp = jnp.concatenate([conv_prev, jnp.concatenate([q_raw, k_raw], axis=-1)], axis=1)
    conv = conv_b
    for j in range(MLSTM_CONV):
        conv = conv + xp[:, j:j + T] * conv_w[j]
    qk = jax.nn.silu(conv)
    q = qk[..., :BRANCH_W]
    k = qk[..., BRANCH_W:] * (MLSTM_HD ** -0.5)
    ig = (i_raw + b_i).astype(F32)
    lf = jax.nn.log_sigmoid((f_raw + b_f).astype(F32))
    out, C, n, m = _mlstm_chunks(q, k, v, o_raw, ig, lf, C0.astype(F32), n0.astype(F32), m0.astype(F32),
                                 norm_w.astype(F32))
    dt = q_raw.dtype
    return out.astype(dt), xp[:, T:], C.astype(dt), n.astype(dt), m.astype(dt)


def _mixers(x, mod, st, p):
    B, T, D = x.shape
    sh1, sc1, g1, sh2, sc2, g2 = jnp.split(mod[:, None, :], 6, axis=-1)
    h = _rmsnorm(x, p['norm1']) * (1.0 + sc1) + sh1
    hb = h.reshape(B * T, D).astype(BF16)
    u_mla, u_rw, u_fox, u_ml, u_gate = [_mm(hb, w).reshape(B, T, w.shape[1]) for w in p['w_sec']]
    cq, ckv, kr = _split(u_mla, [MLA_Q_RANK, MLA_KV_RANK, MLA_ROPE])
    o_mla, lat_new, kr_new = _mla(cq, ckv, kr, st['mla_ckv'], st['mla_krope'], p['mla_q_norm'],
                                  p['mla_kv_norm'], p['mla_w_uq'], p['mla_w_ukv'])
    o_rw, s_new, shift_new = _rwkv7(u_rw, st['rwkv_shift'], st['rwkv_S'], p['rwkv_mu'], p['rwkv_w0'],
                                    p['rwkv_w_lora'], p['rwkv_a0'], p['rwkv_a_lora'], p['rwkv_g_lora'],
                                    p['rwkv_k_k'], p['rwkv_k_a'], p['rwkv_r_k'], p['rwkv_gn_w'], p['rwkv_gn_b'])
    fq, fk, fv, ff = _split(u_fox, [BRANCH_W] * 3 + [FOX_HEADS])
    o_fox, fk_new, fv_new, lf_new = _fox(fq, fk, fv, ff, st['fox_k'], st['fox_v'], st['fox_logf'], p['fox_b_f'])
    mq, mk, mv, mo, mi, mf = _split(u_ml, [BRANCH_W] * 4 + [MLSTM_HEADS] * 2)
    o_ml, conv_new, C_new, n_new, m_new = _mlstm(mq, mk, mv, mo, mi, mf, st['mlstm_conv'], st['mlstm_C'],
                                                 st['mlstm_n'], st['mlstm_m'], p['mlstm_conv_w'],
                                                 p['mlstm_conv_b'], p['mlstm_b_i'], p['mlstm_b_f'],
                                                 p['mlstm_norm_w'])
    o_all = jnp.stack([o.reshape(B * T, BRANCH_W).astype(BF16) for o in (o_mla, o_rw, o_fox, o_ml)])
    mixed = _merge(o_all, p['w_branch_bf'], u_gate.reshape(B * T, GATE_IN))
    x = x + g1 * _mm(mixed, p['w_out_bf']).reshape(B, T, D)
    h2 = _rmsnorm(x, p['norm2']) * (1.0 + sc2) + sh2
    new = {'mla_ckv': lat_new, 'mla_krope': kr_new, 'fox_k': fk_new, 'fox_v': fv_new, 'fox_logf': lf_new,
           'rwkv_S': s_new, 'rwkv_shift': shift_new, 'mlstm_C': C_new, 'mlstm_n': n_new,
           'mlstm_m': m_new, 'mlstm_conv': conv_new}
    return x, h2.reshape(B * T, D).astype(BF16), g2, new


def kernel(x_prompt, x_sample, c_prompt, c_sample,
           cache_mla_ckv, cache_mla_krope, cache_fox_k, cache_fox_v, cache_fox_logf,
           state_rwkv_S, state_rwkv_shift, state_mlstm_C, state_mlstm_n, state_mlstm_m, state_mlstm_conv,
           w_ada, b_ada, norm1, norm2, w_in,
           mla_q_norm, mla_kv_norm, mla_w_uq, mla_w_ukv,
           rwkv_mu, rwkv_w0, rwkv_w_lora, rwkv_a0, rwkv_a_lora, rwkv_g_lora, rwkv_k_k, rwkv_k_a,
           rwkv_r_k, rwkv_gn_w, rwkv_gn_b,
           fox_b_f,
           mlstm_conv_w, mlstm_conv_b, mlstm_b_i, mlstm_b_f, mlstm_norm_w,
           w_branch, w_out, router_w, router_b, expert_w1, expert_b1, expert_w2, expert_b2, final_norm):
    params = dict(b_ada=b_ada, norm1=norm1, norm2=norm2,
                  mla_q_norm=mla_q_norm, mla_kv_norm=mla_kv_norm, mla_w_uq=mla_w_uq, mla_w_ukv=mla_w_ukv,
                  rwkv_mu=rwkv_mu, rwkv_w0=rwkv_w0, rwkv_w_lora=rwkv_w_lora, rwkv_a0=rwkv_a0,
                  rwkv_a_lora=rwkv_a_lora, rwkv_g_lora=rwkv_g_lora, rwkv_k_k=rwkv_k_k, rwkv_k_a=rwkv_k_a,
                  rwkv_r_k=rwkv_r_k, rwkv_gn_w=rwkv_gn_w, rwkv_gn_b=rwkv_gn_b, fox_b_f=fox_b_f,
                  mlstm_conv_w=mlstm_conv_w, mlstm_conv_b=mlstm_conv_b, mlstm_b_i=mlstm_b_i,
                  mlstm_b_f=mlstm_b_f, mlstm_norm_w=mlstm_norm_w,
                  router_w=router_w, router_b=router_b)
    Bp, Tp, D = x_prompt.shape
    Bs, Ts, _ = x_sample.shape
    dt = x_prompt.dtype
    c_all = jnp.concatenate([c_prompt, c_sample], axis=0)
    c_all = jnp.pad(jax.nn.silu(c_all), ((0, -c_all.shape[0] % 16), (0, 0))).astype(BF16)

    prompt_past = dict(
        mla_ckv=jnp.zeros((DEPTH, Bp, 0, MLA_KV_RANK), dt),
        mla_krope=jnp.zeros((DEPTH, Bp, 0, MLA_ROPE), dt),
        fox_k=jnp.zeros((DEPTH, Bp, 0, FOX_HEADS, FOX_HD), dt),
        fox_v=jnp.zeros((DEPTH, Bp, 0, FOX_HEADS, FOX_HD), dt),
        fox_logf=jnp.zeros((DEPTH, Bp, 0, FOX_HEADS), dt),
        rwkv_S=jnp.zeros((DEPTH, Bp, RWKV_HEADS, RWKV_HD, RWKV_HD), dt),
        rwkv_shift=jnp.zeros((DEPTH, Bp, RWKV_IN), dt),
        mlstm_C=jnp.zeros((DEPTH, Bp, MLSTM_HEADS, MLSTM_HD, MLSTM_HD), dt),
        mlstm_n=jnp.zeros((DEPTH, Bp, MLSTM_HEADS, MLSTM_HD), dt),
        mlstm_m=jnp.zeros((DEPTH, Bp, MLSTM_HEADS), dt),
        mlstm_conv=jnp.zeros((DEPTH, Bp, MLSTM_CONV - 1, 2 * BRANCH_W), dt))
    sample_past = dict(
        mla_ckv=cache_mla_ckv, mla_krope=cache_mla_krope, fox_k=cache_fox_k, fox_v=cache_fox_v,
        fox_logf=cache_fox_logf, rwkv_S=state_rwkv_S, rwkv_shift=state_rwkv_shift,
        mlstm_C=state_mlstm_C, mlstm_n=state_mlstm_n, mlstm_m=state_mlstm_m, mlstm_conv=state_mlstm_conv)

    xs = [x_prompt, x_sample]
    cs = [c_prompt, c_sample]
    pasts = [prompt_past, sample_past]
    news = [{}, {}]
    for l in range(DEPTH):
        p = {name: arr[l] for name, arr in params.items()}
        wl = w_in[l]
        p['w_sec'], off = [], 0
        for n in (MLA_IN, RWKV_IN, FOX_IN, MLSTM_IN, GATE_IN):
            p['w_sec'].append(jnp.pad(wl[:, off:off + n], ((0, 0), (0, -n % 512))).astype(BF16))
            off += n
        p['w_branch_bf'] = w_branch[l].astype(BF16)
        p['w_out_bf'] = w_out[l].astype(BF16)
        mod_all = _mm(c_all, w_ada, layer=l) + p['b_ada']
        mods = [mod_all[:Bp], mod_all[Bp:Bp + Bs]]
        mids, h2s, g2s = [], [], []
        for gi in range(2):
            mod = mods[gi]
            st = {name: arr[l] for name, arr in pasts[gi].items()}
            x_mid, h2, g2, ns = _mixers(xs[gi], mod, st, p)
            mids.append(x_mid)
            h2s.append(h2)
            g2s.append(g2)
            for name, arr in ns.items():
                news[gi].setdefault(name, []).append(arr)
        moe_out = _moe(jnp.concatenate(h2s, axis=0), p['router_w'], p['router_b'],
                       expert_w1, expert_b1, expert_w2, expert_b2, l)
        n0 = Bp * Tp
        xs[0] = mids[0] + g2s[0] * moe_out[:n0].reshape(Bp, Tp, D)
        xs[1] = mids[1] + g2s[1] * moe_out[n0:].reshape(Bs, Ts, D)
    y_prompt = _rmsnorm(xs[0], final_norm)
    y_sample = _rmsnorm(xs[1], final_norm)
    sp = {name: jnp.stack(arrs) for name, arrs in news[0].items()}
    ss = {name: jnp.stack(arrs) for name, arrs in news[1].items()}
    return (y_prompt, y_sample,
            sp['mla_ckv'], ss['mla_ckv'], sp['mla_krope'], ss['mla_krope'],
            sp['fox_k'], ss['fox_k'], sp['fox_v'], ss['fox_v'], sp['fox_logf'], ss['fox_logf'],
            sp['rwkv_S'], ss['rwkv_S'], sp['rwkv_shift'], ss['rwkv_shift'],
            sp['mlstm_C'], ss['mlstm_C'], sp['mlstm_n'], ss['mlstm_n'], sp['mlstm_m'], ss['mlstm_m'],
            sp['mlstm_conv'], ss['mlstm_conv'])
```

```python
import functools

import jax
import jax.numpy as jnp
from jax import lax
from jax.experimental import pallas as pl
from jax.experimental.pallas import tpu as pltpu

D_MODEL = 4096
DEPTH = 2
CHUNK = 64
Q_BLOCK = 128
N_BRANCH = 4
BRANCH_W = D_MODEL // N_BRANCH
MLA_HEADS = 8
MLA_NOPE = 128
MLA_ROPE = 64
MLA_V = BRANCH_W // MLA_HEADS
MLA_Q_RANK = 768
MLA_KV_RANK = 512
ROPE_THETA = 10000.0
RWKV_HD = 64
RWKV_HEADS = BRANCH_W // RWKV_HD
RWKV_W_LORA = 64
RWKV_A_LORA = 64
RWKV_G_LORA = 128
RWKV_GN_EPS = 64e-5
RWKV_IN = 3 * BRANCH_W + RWKV_W_LORA + RWKV_A_LORA + RWKV_G_LORA
FOX_HEADS = 8
FOX_HD = BRANCH_W // FOX_HEADS
MLSTM_HEADS = 8
MLSTM_HD = BRANCH_W // MLSTM_HEADS
MLSTM_CONV = 4
N_EXPERTS = 32
TOP_K = 4
EXPERT_FF = D_MODEL // 2
SWIGLU_LIMIT = 7.0
SWIGLU_ALPHA = 1.702
EPS = 1e-6
NEG_INF = -1e30
MLA_IN = MLA_Q_RANK + MLA_KV_RANK + MLA_ROPE
FOX_IN = 3 * BRANCH_W + FOX_HEADS
MLSTM_IN = 4 * BRANCH_W + 2 * MLSTM_HEADS
GATE_IN = N_BRANCH * D_MODEL
IN_W = MLA_IN + RWKV_IN + FOX_IN + MLSTM_IN + GATE_IN

V7X_LANES = 128
V7X_VMEM_LIMIT_BYTES = 56 * 1024 * 1024
BF16 = jnp.bfloat16
F32 = jnp.float32


def _tile(n, pref):
    t = pref
    while t >= 8:
        if n % t == 0:
            return t
        t //= 2
    return n


def _cparams(sem):
    return pltpu.CompilerParams(dimension_semantics=sem, vmem_limit_bytes=V7X_VMEM_LIMIT_BYTES)


def _mm_kernel(a_ref, w_ref, o_ref):
    o_ref[...] = jnp.dot(a_ref[...], w_ref[...].astype(BF16), preferred_element_type=F32)


def _mm(a, w, layer=None, tm_pref=1024, tn_pref=512):
    M, K = a.shape
    N = w.shape[-1]
    tm, tn = _tile(M, tm_pref), _tile(N, tn_pref)
    if layer is None:
        w_spec = pl.BlockSpec((K, tn), lambda i, j: (0, j))
    else:
        w_spec = pl.BlockSpec((None, K, tn), lambda i, j: (layer, 0, j))
    return pl.pallas_call(
        _mm_kernel,
        out_shape=jax.ShapeDtypeStruct((M, N), F32),
        grid=(M // tm, N // tn),
        in_specs=[pl.BlockSpec((tm, K), lambda i, j: (i, 0)), w_spec],
        out_specs=pl.BlockSpec((tm, tn), lambda i, j: (i, j)),
        compiler_params=_cparams(("parallel", "parallel")),
        name="dense_mm",
    )(a, w)


def _merge_kernel(o_ref, wb_ref, ug_ref, out_ref, acc_ref):
    b = pl.program_id(2)

    @pl.when(b == 0)
    def _():
        acc_ref[...] = jnp.zeros_like(acc_ref)

    acc_ref[...] += jax.nn.sigmoid(ug_ref[...]) * jnp.dot(o_ref[0], wb_ref[0], preferred_element_type=F32)

    @pl.when(b == pl.num_programs(2) - 1)
    def _():
        out_ref[...] = acc_ref[...].astype(out_ref.dtype)


def _merge(o_all, wb, u, tm_pref=1024, tn_pref=1024):
    nb, M, W = o_all.shape
    D = wb.shape[2]
    tm, tn = _tile(M, tm_pref), _tile(D, tn_pref)
    npb = D // tn
    return pl.pallas_call(
        _merge_kernel,
        out_shape=jax.ShapeDtypeStruct((M, D), BF16),
        grid=(M // tm, npb, nb),
        in_specs=[pl.BlockSpec((1, tm, W), lambda i, j, b: (b, i, 0)),
                  pl.BlockSpec((1, W, tn), lambda i, j, b: (b, 0, j)),
                  pl.BlockSpec((tm, tn), lambda i, j, b: (i, b * npb + j))],
        out_specs=pl.BlockSpec((tm, tn), lambda i, j, b: (i, j)),
        scratch_shapes=[pltpu.VMEM((tm, tn), F32)],
        compiler_params=_cparams(("parallel", "parallel", "arbitrary")),
        name="branch_merge",
    )(o_all, wb, u)


def _rwkv_scan_kernel(r_ref, w_ref, k_ref, v_ref, kk_ref, a_ref, s0_ref, y_ref, s_ref, kkn_scr, b_scr):
    hd = s_ref.shape[0]
    steps = r_ref.shape[0]
    n_acc = 4

    @pl.when(pl.program_id(1) == 0)
    def _():
        s_ref[...] = s0_ref[...]

    def step(t, carry):
        kkraw = kk_ref[t]
        nrm = jnp.sqrt(jnp.sum(kkraw * kkraw, axis=0, keepdims=True))
        kkn = kkraw / jnp.maximum(nrm, 1e-12)
        kkn_scr[...] = kkn
        b_scr[...] = kkn * a_ref[t]
        acc = [None] * n_acc
        for j in range(hd):
            term = s_ref[j] * kkn_scr[pl.ds(j, 1), :]
            acc[j % n_acc] = term if acc[j % n_acc] is None else acc[j % n_acc] + term
        sa = -functools.reduce(lambda x, y: x + y, acc)
        v = v_ref[t]
        yacc = [None] * n_acc
        for j in range(hd):
            sj = (s_ref[j] * w_ref[t, pl.ds(j, 1), :] + sa * b_scr[pl.ds(j, 1), :]
                  + v * k_ref[t, pl.ds(j, 1), :])
            s_ref[j] = sj
            term = sj * r_ref[t, pl.ds(j, 1), :]
            yacc[j % n_acc] = term if yacc[j % n_acc] is None else yacc[j % n_acc] + term
        y_ref[t] = functools.reduce(lambda x, y: x + y, yacc)
        return carry

    lax.fori_loop(0, steps, step, 0)


def _rwkv_scan(r, w, k, v, kk, a, s0, tb_pref=32):
    T, hd, NL = r.shape
    tb = _tile(T, tb_pref)
    seq = pl.BlockSpec((tb, hd, V7X_LANES), lambda g, t: (t, 0, g))
    st = pl.BlockSpec((hd, hd, V7X_LANES), lambda g, t: (0, 0, g))
    return pl.pallas_call(
        _rwkv_scan_kernel,
        out_shape=(jax.ShapeDtypeStruct((T, hd, NL), F32), jax.ShapeDtypeStruct((hd, hd, NL), F32)),
        grid=(NL // V7X_LANES, T // tb),
        in_specs=[seq] * 6 + [st],
        out_specs=(seq, st),
        scratch_shapes=[pltpu.VMEM((hd, V7X_LANES), F32), pltpu.VMEM((hd, V7X_LANES), F32)],
        compiler_params=_cparams(("parallel", "arbitrary")),
        name="rwkv_scan",
    )(r, w, k, v, kk, a, s0)


def _expert_kernel(te_ref, tv_ref, x_ref, w1g_ref, w1u_ref, b1g_ref, b1u_ref, w2_ref, b2_ref, rw_ref, o_ref,
                   act_scr):
    i, s = pl.program_id(0), pl.program_id(1)
    nf, tm, tf = act_scr.shape
    th = tm // 2
    valid, second = tv_ref[i] >= 1, tv_ref[i] == 2

    @pl.when(jnp.logical_and(valid, s < nf))
    def _():
        wg, wu = w1g_ref[0].astype(BF16), w1u_ref[0].astype(BF16)

        def half(lo):
            x = x_ref[lo:lo + th, :]
            g = jnp.dot(x, wg, preferred_element_type=F32) + b1g_ref[0]
            u = jnp.dot(x, wu, preferred_element_type=F32) + b1u_ref[0]
            g = jnp.minimum(g, SWIGLU_LIMIT)
            u = jnp.clip(u, -SWIGLU_LIMIT, SWIGLU_LIMIT)
            act_scr[s, lo:lo + th, :] = ((u + 1.0) * g * jax.nn.sigmoid(SWIGLU_ALPHA * g)).astype(BF16)

        half(0)
        pl.when(second)(lambda: half(th))

    @pl.when(jnp.logical_and(valid, s >= nf))
    def _():
        w2 = [w2_ref[0, j * tf:(j + 1) * tf, :].astype(BF16) for j in range(nf)]

        def half(lo):
            acc = None
            for j in range(nf):
                part = jnp.dot(act_scr[j, lo:lo + th, :], w2[j], preferred_element_type=F32)
                acc = part if acc is None else acc + part
            o_ref[lo:lo + th, :] = (acc + b2_ref[0]) * rw_ref[lo:lo + th, :]

        half(0)
        pl.when(second)(lambda: half(th))

    @pl.when(jnp.logical_and(jnp.logical_not(valid), s >= nf))
    def _():
        o_ref[:th, :] = jnp.zeros((th, o_ref.shape[1]), o_ref.dtype)

    @pl.when(jnp.logical_and(jnp.logical_not(second), s >= nf))
    def _():
        o_ref[th:, :] = jnp.zeros((th, o_ref.shape[1]), o_ref.dtype)


def _experts(xg, w1, b1, w2, b2, roww, tile_e, tile_v, tm, tf_pref=256, tn_pref=512):
    R, D = xg.shape
    F = w2.shape[1]
    tf, tn = _tile(F, tf_pref), _tile(D, tn_pref)
    nf, nn = F // tf, D // tn

    def c1(i, s, tv):
        return jnp.where(tv[i] >= 1, jnp.minimum(s, nf - 1), nf - 1)

    def c2(i, s, tv):
        return jnp.where(tv[i] >= 1, jnp.maximum(s - nf, 0), nn - 1)

    grid_spec = pltpu.PrefetchScalarGridSpec(
        num_scalar_prefetch=2,
        grid=(R // tm, nf + nn),
        in_specs=[
            pl.BlockSpec((tm, D), lambda i, s, te, tv: (i, 0), pipeline_mode=pl.Buffered(1)),
            pl.BlockSpec((1, D, tf), lambda i, s, te, tv: (te[i], 0, c1(i, s, tv))),
            pl.BlockSpec((1, D, tf), lambda i, s, te, tv: (te[i], 0, nf + c1(i, s, tv))),
            pl.BlockSpec((1, 1, tf), lambda i, s, te, tv: (te[i], 0, c1(i, s, tv))),
            pl.BlockSpec((1, 1, tf), lambda i, s, te, tv: (te[i], 0, nf + c1(i, s, tv))),
            pl.BlockSpec((1, F, tn), lambda i, s, te, tv: (te[i], 0, c2(i, s, tv))),
            pl.BlockSpec((1, 1, tn), lambda i, s, te, tv: (te[i], 0, c2(i, s, tv))),
            pl.BlockSpec((tm, 1), lambda i, s, te, tv: (i, 0)),
        ],
        out_specs=pl.BlockSpec((tm, tn), lambda i, s, te, tv: (i, jnp.maximum(s - nf, 0))),
        scratch_shapes=[pltpu.VMEM((nf, tm, tf), BF16)],
    )
    return pl.pallas_call(
        _expert_kernel,
        out_shape=jax.ShapeDtypeStruct((R, D), F32),
        grid_spec=grid_spec,
        compiler_params=_cparams(("parallel", "arbitrary")),
        name="moe_experts",
    )(tile_e, tile_v, xg, w1, w1, b1, b1, w2, b2, roww)


def _moe(x, router_w, router_b, w1, b1, w2, b2, layer, tm_pref=1024):
    N, D = x.shape
    E = router_w.shape[1]
    w1 = w1.reshape((-1,) + w1.shape[2:])
    w2 = w2.reshape((-1,) + w2.shape[2:])
    b1 = b1.reshape(-1, 1, b1.shape[-1])
    b2 = b2.reshape(-1, 1, b2.shape[-1])
    logits = jnp.dot(x.astype(F32), router_w, precision=lax.Precision.HIGHEST) + router_b
    top_v, top_i = lax.top_k(logits, TOP_K)
    top_w = jax.nn.softmax(top_v, axis=-1)
    npair = N * TOP_K
    tm = _tile(npair, tm_pref)
    R = npair + E * tm
    e_flat = top_i.reshape(-1).astype(jnp.int32)
    onehot = (e_flat[:, None] == jnp.arange(E, dtype=jnp.int32)[None, :]).astype(jnp.int32)
    csum = jnp.cumsum(onehot, axis=0)
    counts = csum[-1]
    rank = jnp.sum((csum - onehot) * onehot, axis=1)
    starts = jnp.cumsum(counts) - counts
    padded = ((counts + tm - 1) // tm) * tm
    pend = jnp.cumsum(padded)
    pstart = pend - padded
    pos = pstart[e_flat] + rank
    order = jnp.argsort(e_flat, stable=True).astype(jnp.int32)
    rows = jnp.arange(R, dtype=jnp.int32)
    half_start = jnp.arange(R // (tm // 2), dtype=jnp.int32) * (tm // 2)
    half_e = jnp.minimum(jnp.sum((half_start[:, None] >= pend[None, :]).astype(jnp.int32), axis=1), E - 1)
    row_e = jnp.repeat(half_e, tm // 2)
    row_rank = rows - pstart[row_e]
    row_ok = jnp.logical_and(rows < pend[-1], row_rank < counts[row_e])
    row_pair = order[jnp.clip(starts[row_e] + row_rank, 0, npair - 1)]
    src_tok = jnp.where(row_ok, row_pair // TOP_K, rows % N)
    roww = jnp.where(row_ok, top_w.reshape(-1)[row_pair], 0.0)
    tile_e = row_e[::tm]
    tile_v = jnp.sum(row_ok[::tm // 2].astype(jnp.int32).reshape(-1, 2), axis=1)
    last_e = tile_e[jnp.maximum(jnp.sum((tile_v >= 1).astype(jnp.int32)) - 1, 0)]
    tile_e = jnp.where(tile_v >= 1, tile_e, last_e) + layer * E
    xg = jnp.take(x, src_tok, axis=0, mode="clip")
    y = _experts(xg, w1, b1, w2, b2, roww[:, None], tile_e, tile_v, tm)
    pos = pos.reshape(N, TOP_K)
    parts = [jnp.take(y, pos[:, s], axis=0, mode="clip") for s in range(TOP_K)]
    return functools.reduce(lambda a, b: a + b, parts)


def _attn_kernel(*refs, scale, tk, q_off, tk_valid, chunk_mask, use_bias):
    if use_bias:
        q_ref, k_ref, v_ref, fq_ref, fk_ref, o_ref, m_scr, l_scr, acc_scr = refs
    else:
        q_ref, k_ref, v_ref, o_ref, m_scr, l_scr, acc_scr = refs
    tq = q_ref.shape[2]
    n_kb_total = k_ref.shape[2] // tk
    qi = pl.program_id(2)
    q = q_ref[0, 0]
    m_scr[...] = jnp.full_like(m_scr, NEG_INF)
    l_scr[...] = jnp.zeros_like(l_scr)
    acc_scr[...] = jnp.zeros_like(acc_scr)
    q_pos = q_off + qi * tq + lax.broadcasted_iota(jnp.int32, (tq, tk), 0)
    last_q = q_off + (qi + 1) * tq - 1
    limit = (last_q // CHUNK + 1) * CHUNK if chunk_mask else last_q + 1
    n_kb = jnp.minimum((limit + tk - 1) // tk, n_kb_total)
    first_q = q_off + qi * tq
    all_visible = (first_q // CHUNK + 1) * CHUNK if chunk_mask else first_q + 1
    n_full = jnp.minimum(jnp.minimum(all_visible, tk_valid) // tk, n_kb)

    def body(kb, carry, masked):
        start = pl.multiple_of(kb * tk, tk)
        k = k_ref[0, 0, pl.ds(start, tk), :]
        v = v_ref[0, 0, pl.ds(start, tk), :]
        s = lax.dot_general(q, k, (((1,), (1,)), ((), ())), preferred_element_type=F32) * scale
        if use_bias:
            s = s + (fq_ref[0, 0] - fk_ref[0, 0, kb])
        if masked:
            k_pos = kb * tk + lax.broadcasted_iota(jnp.int32, (tq, tk), 1)
            ok = (k_pos // CHUNK <= q_pos // CHUNK) if chunk_mask else (k_pos <= q_pos)
            ok = jnp.logical_and(ok, k_pos < tk_valid)
            s = jnp.where(ok, s, NEG_INF)
        m_old = m_scr[...]
        m_new = jnp.maximum(m_old, jnp.max(s, axis=1, keepdims=True))
        alpha = jnp.exp(m_old - m_new)
        p = jnp.exp(s - m_new)
        l_scr[...] = alpha * l_scr[...] + jnp.sum(p, axis=1, keepdims=True)
        acc_scr[...] = alpha * acc_scr[...] + jnp.dot(p.astype(v.dtype), v, preferred_element_type=F32)
        m_scr[...] = m_new
        return carry

    lax.fori_loop(0, n_full, functools.partial(body, masked=False), 0)
    lax.fori_loop(n_full, n_kb, functools.partial(body, masked=True), 0)
    o_ref[0, 0] = acc_scr[...] / l_scr[...]


def _attention(q, k, v, *, tk_valid, chunk_mask, fq=None, fk=None, tq_pref=512, tk_pref=512):
    B, H, Tq, dq = q.shape
    Tk, dv = k.shape[2], v.shape[3]
    tq = _tile(Tq, tq_pref)
    tk = Tk if tq * Tk * 4 <= 512 * 1024 else _tile(Tk, tk_pref)
    use_bias = fq is not None
    in_specs = [pl.BlockSpec((1, 1, tq, dq), lambda b, h, i: (b, h, i, 0)),
                pl.BlockSpec((1, 1, Tk, dq), lambda b, h, i: (b, h, 0, 0)),
                pl.BlockSpec((1, 1, Tk, dv), lambda b, h, i: (b, h, 0, 0))]
    args = [q, k, v]
    if use_bias:
        in_specs += [pl.BlockSpec((1, 1, tq, 1), lambda b, h, i: (b, h, i, 0)),
                     pl.BlockSpec((1, 1, Tk // tk, 1, tk), lambda b, h, i: (b, h, 0, 0, 0))]
        args += [fq[..., None], fk.reshape(B, H, Tk // tk, 1, tk)]
    return pl.pallas_call(
        functools.partial(_attn_kernel, scale=dq ** -0.5, tk=tk, q_off=tk_valid - Tq, tk_valid=tk_valid,
                          chunk_mask=chunk_mask, use_bias=use_bias),
        out_shape=jax.ShapeDtypeStruct((B, H, Tq, dv), F32),
        grid=(B, H, Tq // tq),
        in_specs=in_specs,
        out_specs=pl.BlockSpec((1, 1, tq, dv), lambda b, h, i: (b, h, i, 0)),
        scratch_shapes=[pltpu.VMEM((tq, 1), F32), pltpu.VMEM((tq, 1), F32), pltpu.VMEM((tq, dv), F32)],
        compiler_params=_cparams(("parallel", "parallel", "arbitrary")),
        name="causal_attention",
    )(*args)


def _mlstm_kernel(q_ref, k_ref, v_ref, o_ref, bcol_ref, brow_ref, igcol_ref, igrow_ref, c0_ref, n0_ref, m0_ref,
                  nw_ref, out_ref, c_ref, n_ref, m_ref):
    H, d = c_ref.shape[1], c_ref.shape[2]
    L = q_ref.shape[1]

    @pl.when(pl.program_id(1) == 0)
    def _():
        c_ref[...] = c0_ref[...]
        n_ref[...] = n0_ref[...]
        m_ref[...] = m0_ref[...]

    tri = lax.broadcasted_iota(jnp.int32, (L, L), 1) <= lax.broadcasted_iota(jnp.int32, (L, L), 0)
    nt = (((1,), (1,)), ((), ()))
    for h in range(H):
        sl = slice(h * d, (h + 1) * d)
        q, k, v = q_ref[0, :, sl], k_ref[0, :, sl], v_ref[0, :, sl]
        bcol, brow = bcol_ref[0, h], brow_ref[0, h, 0]
        igcol, igrow = igcol_ref[0, h], igrow_ref[0, h, 0]
        mp = m_ref[0, h][:, :1]
        dlog = jnp.where(tri, bcol - brow + igrow, NEG_INF)
        inter = bcol + mp
        m_t = jnp.maximum(inter, jnp.max(dlog, axis=1, keepdims=True))
        dw = jnp.exp(dlog - m_t)
        iw = jnp.exp(inter - m_t)
        qb, kb, vb = q.astype(BF16), k.astype(BF16), v.astype(BF16)
        sw = lax.dot_general(qb, kb, nt, preferred_element_type=F32) * dw
        C, n = c_ref[0, h], n_ref[0, h]
        num = (jnp.dot(sw.astype(BF16), vb, preferred_element_type=F32)
               + iw * lax.dot_general(qb, C.astype(BF16), nt, preferred_element_type=F32))
        den = jnp.sum(sw, axis=1, keepdims=True) + iw * jnp.sum(q * n, axis=1, keepdims=True)
        hh = num / jnp.maximum(jnp.abs(den), jnp.exp(-m_t))
        m_new = m_t[L - 1:L, :]
        b_last = bcol[L - 1:L, :]
        wscol = jnp.exp(b_last - bcol + igcol - m_new)
        cw = jnp.exp(b_last + mp - m_new)
        wv_t = jnp.transpose(wscol * v).astype(BF16)
        c_ref[0, h] = cw * C + jnp.dot(wv_t, kb, preferred_element_type=F32)
        n_ref[0, h] = cw * n + jnp.sum(wscol * k, axis=0, keepdims=True)
        m_ref[0, h] = jnp.broadcast_to(m_new, (1, d))
        mu = jnp.mean(hh, axis=1, keepdims=True)
        var = jnp.mean(jnp.square(hh - mu), axis=1, keepdims=True)
        y = (hh - mu) * lax.rsqrt(var + EPS) * nw_ref[:, sl]
        out_ref[0, :, sl] = jax.nn.sigmoid(o_ref[0, :, sl]) * y


def _mlstm_chunks(q, k, v, o_raw, ig, lf, C0, n0, m0, norm_w):
    B, T, W = q.shape
    H = ig.shape[2]
    d = W // H
    L = CHUNK if T % CHUNK == 0 else T
    nc = T // L
    b = jnp.cumsum(lf.reshape(B, nc, L, H), axis=2).reshape(B, T, H)
    bt, igt = jnp.swapaxes(b, 1, 2), jnp.swapaxes(ig, 1, 2)
    seq = pl.BlockSpec((1, L, W), lambda i, c: (i, c, 0))
    col = pl.BlockSpec((1, H, L, 1), lambda i, c: (i, 0, c, 0))
    row = pl.BlockSpec((1, H, 1, 1, L), lambda i, c: (i, 0, c, 0, 0))
    rows = [bt.reshape(B, H, nc, 1, L), igt.reshape(B, H, nc, 1, L)]
    stC = pl.BlockSpec((1, H, d, d), lambda i, c: (i, 0, 0, 0))
    stn = pl.BlockSpec((1, H, 1, d), lambda i, c: (i, 0, 0, 0))
    m0b = jnp.broadcast_to(m0[:, :, None, None], (B, H, 1, d))
    out, C, n, m = pl.pallas_call(
        _mlstm_kernel,
        out_shape=(jax.ShapeDtypeStruct((B, T, W), F32), jax.ShapeDtypeStruct((B, H, d, d), F32),
                   jax.ShapeDtypeStruct((B, H, 1, d), F32), jax.ShapeDtypeStruct((B, H, 1, d), F32)),
        grid=(B, nc),
        in_specs=[seq, seq, seq, seq, col, row, col, row, stC, stn, stn,
                  pl.BlockSpec((1, W), lambda i, c: (0, 0))],
        out_specs=(seq, stC, stn, stn),
        compiler_params=_cparams(("parallel", "arbitrary")),
        name="mlstm_chunks",
    )(q, k, v, o_raw, bt[..., None], rows[0], igt[..., None], rows[1], C0, n0[:, :, None, :], m0b,
      norm_w[None, :])
    return out, C, n[:, :, 0, :], m[:, :, 0, 0]


def _split(u, sizes):
    out, off = [], 0
    for s in sizes:
        out.append(u[..., off:off + s])
        off += s
    return out


def _rmsnorm(x, g):
    xf = x.astype(F32)
    y = xf * lax.rsqrt(jnp.mean(xf * xf, axis=-1, keepdims=True) + EPS)
    return (y * g.astype(F32)).astype(x.dtype)


def _headnorm(y, eps):
    mu = jnp.mean(y, axis=-1, keepdims=True)
    var = jnp.mean(jnp.square(y - mu), axis=-1, keepdims=True)
    return (y - mu) * lax.rsqrt(var + eps)


def _chunk_causal(tq, tk):
    return (tk // CHUNK) <= (tq // CHUNK)


def _frame_causal(tq, tk):
    return tk <= tq


def _rope(x, pos):
    half = MLA_ROPE // 2
    inv = ROPE_THETA ** (-jnp.arange(half, dtype=F32) / half)
    ang = pos.astype(F32)[:, None] * inv[None, :]
    shape = (1, pos.shape[0]) + (1,) * (x.ndim - 3) + (half,)
    cos = jnp.cos(ang).reshape(shape)
    sin = jnp.sin(ang).reshape(shape)
    xf = x.astype(F32)
    x1, x2 = xf[..., :half], xf[..., half:]
    return jnp.concatenate([x1 * cos - x2 * sin, x2 * cos + x1 * sin], axis=-1).astype(x.dtype)


def _head_major(z, tk_pad=None):
    z = jnp.transpose(z, (0, 2, 1, 3)).astype(BF16)
    if tk_pad is not None and tk_pad != z.shape[2]:
        z = jnp.pad(z, ((0, 0), (0, 0), (0, tk_pad - z.shape[2]), (0, 0)))
    return z


def _mla(cq, ckv, kr, past_ckv, past_kr, q_norm, kv_norm, w_uq, w_ukv):
    B, T, _ = cq.shape
    P = past_ckv.shape[1]
    Tk = P + T
    tk_pad = -(-Tk // V7X_LANES) * V7X_LANES
    q_pos = P + jnp.arange(T, dtype=jnp.int32)
    lat_new = _rmsnorm(ckv, kv_norm)
    kr_new = _rope(kr, q_pos)
    q = _mm(_rmsnorm(cq, q_norm).reshape(B * T, MLA_Q_RANK).astype(BF16), w_uq.astype(BF16))
    q = q.reshape(B, T, MLA_HEADS, MLA_NOPE + MLA_ROPE)
    q = jnp.concatenate([q[..., :MLA_NOPE], _rope(q[..., MLA_NOPE:], q_pos)], axis=-1)
    lat = jnp.concatenate([past_ckv, lat_new], axis=1)
    k_rope = jnp.concatenate([past_kr, kr_new], axis=1)
    kv = _mm(lat.reshape(B * Tk, MLA_KV_RANK).astype(BF16), w_ukv.astype(BF16),
             tm_pref=768 if (B * Tk) % 768 == 0 else 1024, tn_pref=2048)
    kv = kv.reshape(B, Tk, MLA_HEADS, MLA_NOPE + MLA_V)
    k = jnp.concatenate([kv[..., :MLA_NOPE],
                         jnp.broadcast_to(k_rope[:, :, None, :], (B, Tk, MLA_HEADS, MLA_ROPE))], axis=-1)
    o = _attention(_head_major(q), _head_major(k, tk_pad), _head_major(kv[..., MLA_NOPE:], tk_pad),
                   tk_valid=Tk, chunk_mask=True)
    return jnp.transpose(o, (0, 2, 1, 3)).reshape(B, T, BRANCH_W), lat_new, kr_new


def _rwkv7(u, prev_u, S0, mu, w0, w_lora, a0, a_lora, g_lora, k_k, k_a, r_k, gn_w, gn_b):
    B, T, _ = u.shape
    H, hd = RWKV_HEADS, RWKV_HD
    pad = u.shape[-1] - RWKV_IN
    prev_u = jnp.pad(prev_u, ((0, 0), (0, pad)))
    u_prev = jnp.concatenate([prev_u[:, None, :], u[:, :-1]], axis=1)
    um = u + (u_prev - u) * jnp.pad(mu, (0, pad))
    r, k, v, wd, ad, gd = _split(um, [BRANCH_W] * 3 + [RWKV_W_LORA, RWKV_A_LORA, RWKV_G_LORA])
    W = BRANCH_W
    z = jnp.concatenate([jnp.tanh(wd), ad, jax.nn.sigmoid(gd)], axis=-1)
    zw = z.shape[-1]
    w_bd = jnp.zeros((zw, 3 * W), F32)
    w_bd = w_bd.at[:RWKV_W_LORA, :W].set(w_lora)
    w_bd = w_bd.at[RWKV_W_LORA:RWKV_W_LORA + RWKV_A_LORA, W:2 * W].set(a_lora)
    w_bd = w_bd.at[RWKV_W_LORA + RWKV_A_LORA:, 2 * W:].set(g_lora)
    lo = _mm(z.reshape(B * T, zw).astype(BF16), w_bd.astype(BF16)).reshape(B, T, 3 * W)
    w_raw = (w0 + lo[..., :W]).astype(F32)
    decay = jnp.exp(-jnp.exp(-jax.nn.softplus(-w_raw) - 0.5))
    a = jax.nn.sigmoid((a0 + lo[..., W:2 * W]).astype(F32))
    g = lo[..., 2 * W:]
    kk = k * k_k
    k = k.astype(F32) * (1.0 + (a - 1.0) * k_a.astype(F32))

    nl = B * H
    nl_pad = -(-nl // V7X_LANES) * V7X_LANES

    def lanes(z):
        z = jnp.transpose(z.reshape(B, T, H, hd), (1, 3, 0, 2)).reshape(T, hd, nl)
        return jnp.pad(z, ((0, 0), (0, 0), (0, nl_pad - nl)))

    s0 = jnp.transpose(S0.astype(F32), (3, 2, 0, 1)).reshape(hd, hd, nl)
    s0 = jnp.pad(s0, ((0, 0), (0, 0), (0, nl_pad - nl)))
    ys, s_fin = _rwkv_scan(lanes(r), lanes(decay), lanes(k), lanes(v), lanes(kk), lanes(a), s0)
    ys = jnp.transpose(ys[:, :, :nl].reshape(T, hd, B, H), (2, 0, 3, 1))
    s_fin = jnp.transpose(s_fin[:, :, :nl].reshape(hd, hd, B, H), (2, 3, 1, 0))

    def heads(z):
        return z.astype(F32).reshape(B, T, H, hd)

    rh, kh, vh = heads(r), heads(k), heads(v)
    y = _headnorm(ys, RWKV_GN_EPS).reshape(B, T, BRANCH_W) * gn_w.astype(F32) + gn_b.astype(F32)
    bonus = jnp.sum(rh * kh * r_k.astype(F32), axis=-1, keepdims=True) * vh
    out = (y + bonus.reshape(B, T, BRANCH_W)) * g.astype(F32)
    return out.astype(u.dtype), s_fin.astype(u.dtype), u[:, -1, :RWKV_IN]


def _fox(q, k, v, f_raw, past_k, past_v, past_logf, b_f):
    B, T, _ = q.shape
    P = past_k.shape[1]

    def hd(z):
        return z.reshape(B, T, FOX_HEADS, FOX_HD)

    qh, kh, vh = hd(q), hd(k), hd(v)
    logf = jax.nn.log_sigmoid((f_raw + b_f).astype(F32))
    F = jnp.cumsum(jnp.concatenate([past_logf.astype(F32), logf], axis=1), axis=1)
    Tk = P + T
    tk_pad = -(-Tk // V7X_LANES) * V7X_LANES
    K = jnp.concatenate([past_k, kh], axis=1)
    V = jnp.concatenate([past_v, vh], axis=1)
    Ft = jnp.swapaxes(F, 1, 2)
    o = _attention(_head_major(qh), _head_major(K, tk_pad), _head_major(V, tk_pad),
                   tk_valid=Tk, chunk_mask=False,
                   fq=Ft[:, :, P:], fk=jnp.pad(Ft, ((0, 0), (0, 0), (0, tk_pad - Tk))))
    return jnp.transpose(o, (0, 2, 1, 3)).reshape(B, T, BRANCH_W), kh, vh, logf.astype(q.dtype)


def _mlstm(q_raw, k_raw, v, o_raw, i_raw, f_raw, conv_prev, C0, n0, m0, conv_w, conv_b, b_i, b_f, norm_w):
    B, T, _ = q_raw.shape
    xp = jnp.concatenate([conv_prev, jnp.concatenate([q_raw, k_raw], axis=-1)], axis=1)
    conv = conv_b
    for j in range(MLSTM_CONV):
        conv = conv + xp[:, j:j + T] * conv_w[j]
    qk = jax.nn.silu(conv)
    q = qk[..., :BRANCH_W]
    k = qk[..., BRANCH_W:] * (MLSTM_HD ** -0.5)
    ig = (i_raw + b_i).astype(F32)
    lf = jax.nn.log_sigmoid((f_raw + b_f).astype(F32))
    out, C, n, m = _mlstm_chunks(q, k, v, o_raw, ig, lf, C0.astype(F32), n0.astype(F32), m0.astype(F32),
                                 norm_w.astype(F32))
    dt = q_raw.dtype
    return out.astype(dt), xp[:, T:], C.astype(dt), n.astype(dt), m.astype(dt)


def _mixers(x, mod, st, p):
    B, T, D = x.shape
    sh1, sc1, g1, sh2, sc2, g2 = jnp.split(mod[:, None, :], 6, axis=-1)
    h = _rmsnorm(x, p['norm1']) * (1.0 + sc1) + sh1
    hb = h.reshape(B * T, D).astype(BF16)
    u_mla, u_rw, u_fox, u_ml, u_gate = [_mm(hb, w).reshape(B, T, w.shape[1]) for w in p['w_sec']]
    cq, ckv, kr = _split(u_mla, [MLA_Q_RANK, MLA_KV_RANK, MLA_ROPE])
    o_mla, lat_new, kr_new = _mla(cq, ckv, kr, st['mla_ckv'], st['mla_krope'], p['mla_q_norm'],
                                  p['mla_kv_norm'], p['mla_w_uq'], p['mla_w_ukv'])
    o_rw, s_new, shift_new = _rwkv7(u_rw, st['rwkv_shift'], st['rwkv_S'], p['rwkv_mu'], p['rwkv_w0'],
                                    p['rwkv_w_lora'], p['rwkv_a0'], p['rwkv_a_lora'], p['rwkv_g_lora'],
                                    p['rwkv_k_k'], p['rwkv_k_a'], p['rwkv_r_k'], p['rwkv_gn_w'], p['rwkv_gn_b'])
    fq, fk, fv, ff = _split(u_fox, [BRANCH_W] * 3 + [FOX_HEADS])
    o_fox, fk_new, fv_new, lf_new = _fox(fq, fk, fv, ff, st['fox_k'], st['fox_v'], st['fox_logf'], p['fox_b_f'])
    mq, mk, mv, mo, mi, mf = _split(u_ml, [BRANCH_W] * 4 + [MLSTM_HEADS] * 2)
    o_ml, conv_new, C_new, n_new, m_new = _mlstm(mq, mk, mv, mo, mi, mf, st['mlstm_conv'], st['mlstm_C'],
                                                 st['mlstm_n'], st['mlstm_m'], p['mlstm_conv_w'],
                                                 p['mlstm_conv_b'], p['mlstm_b_i'], p['mlstm_b_f'],
                                                 p['mlstm_norm_w'])
    o_all = jnp.stack([o.reshape(B * T, BRANCH_W).astype(BF16) for o in (o_mla, o_rw, o_fox, o_ml)])
    mixed = _merge(o_all, p['w_branch_bf'], u_gate.reshape(B * T, GATE_IN))
    x = x + g1 * _mm(mixed, p['w_out_bf']).reshape(B, T, D)
    h2 = _rmsnorm(x, p['norm2']) * (1.0 + sc2) + sh2
    new = {'mla_ckv': lat_new, 'mla_krope': kr_new, 'fox_k': fk_new, 'fox_v': fv_new, 'fox_logf': lf_new,
           'rwkv_S': s_new, 'rwkv_shift': shift_new, 'mlstm_C': C_new, 'mlstm_n': n_new,
           'mlstm_m': m_new, 'mlstm_conv': conv_new}
    return x, h2.reshape(B * T, D).astype(BF16), g2, new


def kernel(x_prompt, x_sample, c_prompt, c_sample,
           cache_mla_ckv, cache_mla_krope, cache_fox_k, cache_fox_v, cache_fox_logf,
           state_rwkv_S, state_rwkv_shift, state_mlstm_C, state_mlstm_n, state_mlstm_m, state_mlstm_conv,
           w_ada, b_ada, norm1, norm2, w_in,
           mla_q_norm, mla_kv_norm, mla_w_uq, mla_w_ukv,
           rwkv_mu, rwkv_w0, rwkv_w_lora, rwkv_a0, rwkv_a_lora, rwkv_g_lora, rwkv_k_k, rwkv_k_a,
           rwkv_r_k, rwkv_gn_w, rwkv_gn_b,
           fox_b_f,
           mlstm_conv_w, mlstm_conv_b, mlstm_b_i, mlstm_b_f, mlstm_norm_w,
           w_branch, w_out, router_w, router_b, expert_w1, expert_b1, expert_w2, expert_b2, final_norm):
    params = dict(b_ada=b_ada, norm1=norm1, norm2=norm2,
                  mla_q_norm=mla_q_norm, mla_kv_norm=mla_kv_norm, mla_w_uq=mla_w_uq, mla_w_ukv=mla_w_ukv,
                  rwkv_mu=rwkv_mu, rwkv_w0=rwkv_w0, rwkv_w_lora=rwkv_w_lora, rwkv_a0=rwkv_a0,
                  rwkv_a_lora=rwkv_a_lora, rwkv_g_lora=rwkv_g_lora, rwkv_k_k=rwkv_k_k, rwkv_k_a=rwkv_k_a,
                  rwkv_r_k=rwkv_r_k, rwkv_gn_w=rwkv_gn_w, rwkv_gn_b=rwkv_gn_b, fox_b_f=fox_b_f,
                  mlstm_conv_w=mlstm_conv_w, mlstm_conv_b=mlstm_conv_b, mlstm_b_i=mlstm_b_i,
                  mlstm_b_f=mlstm_b_f, mlstm_norm_w=mlstm_norm_w,
                  router_w=router_w, router_b=router_b)
    Bp, Tp, D = x_prompt.shape
    Bs, Ts, _ = x_sample.shape
    dt = x_prompt.dtype
    c_all = jnp.concatenate([c_prompt, c_sample], axis=0)
    c_all = jnp.pad(jax.nn.silu(c_all), ((0, -c_all.shape[0] % 16), (0, 0))).astype(BF16)

    prompt_past = dict(
        mla_ckv=jnp.zeros((DEPTH, Bp, 0, MLA_KV_RANK), dt),
        mla_krope=jnp.zeros((DEPTH, Bp, 0, MLA_ROPE), dt),
        fox_k=jnp.zeros((DEPTH, Bp, 0, FOX_HEADS, FOX_HD), dt),
        fox_v=jnp.zeros((DEPTH, Bp, 0, FOX_HEADS, FOX_HD), dt),
        fox_logf=jnp.zeros((DEPTH, Bp, 0, FOX_HEADS), dt),
        rwkv_S=jnp.zeros((DEPTH, Bp, RWKV_HEADS, RWKV_HD, RWKV_HD), dt),
        rwkv_shift=jnp.zeros((DEPTH, Bp, RWKV_IN), dt),
        mlstm_C=jnp.zeros((DEPTH, Bp, MLSTM_HEADS, MLSTM_HD, MLSTM_HD), dt),
        mlstm_n=jnp.zeros((DEPTH, Bp, MLSTM_HEADS, MLSTM_HD), dt),
        mlstm_m=jnp.zeros((DEPTH, Bp, MLSTM_HEADS), dt),
        mlstm_conv=jnp.zeros((DEPTH, Bp, MLSTM_CONV - 1, 2 * BRANCH_W), dt))
    sample_past = dict(
        mla_ckv=cache_mla_ckv, mla_krope=cache_mla_krope, fox_k=cache_fox_k, fox_v=cache_fox_v,
        fox_logf=cache_fox_logf, rwkv_S=state_rwkv_S, rwkv_shift=state_rwkv_shift,
        mlstm_C=state_mlstm_C, mlstm_n=state_mlstm_n, mlstm_m=state_mlstm_m, mlstm_conv=state_mlstm_conv)

    xs = [x_prompt, x_sample]
    cs = [c_prompt, c_sample]
    pasts = [prompt_past, sample_past]
    news = [{}, {}]
    for l in range(DEPTH):
        p = {name: arr[l] for name, arr in params.items()}
        wl = w_in[l]
        p['w_sec'], off = [], 0
        for n in (MLA_IN, RWKV_IN, FOX_IN, MLSTM_IN, GATE_IN):
            p['w_sec'].append(jnp.pad(wl[:, off:off + n], ((0, 0), (0, -n % 512))).astype(BF16))
            off += n
        p['w_branch_bf'] = w_branch[l].astype(BF16)
        p['w_out_bf'] = w_out[l].astype(BF16)
        mod_all = _mm(c_all, w_ada, layer=l) + p['b_ada']
        mods = [mod_all[:Bp], mod_all[Bp:Bp + Bs]]
        mids, h2s, g2s = [], [], []
        for gi in range(2):
            mod = mods[gi]
            st = {name: arr[l] for name, arr in pasts[gi].items()}
            x_mid, h2, g2, ns = _mixers(xs[gi], mod, st, p)
            mids.append(x_mid)
            h2s.append(h2)
            g2s.append(g2)
            for name, arr in ns.items():
                news[gi].setdefault(name, []).append(arr)
        moe_out = _moe(jnp.concatenate(h2s, axis=0), p['router_w'], p['router_b'],
                       expert_w1, expert_b1, expert_w2, expert_b2, l)
        n0 = Bp * Tp
        xs[0] = mids[0] + g2s[0] * moe_out[:n0].reshape(Bp, Tp, D)
        xs[1] = mids[1] + g2s[1] * moe_out[n0:].reshape(Bs, Ts, D)
    y_prompt = _rmsnorm(xs[0], final_norm)
    y_sample = _rmsnorm(xs[1], final_norm)
    sp = {name: jnp.stack(arrs) for name, arrs in news[0].items()}
    ss = {name: jnp.stack(arrs) for name, arrs in news[1].items()}
    return (y_prompt, y_sample,
            sp['mla_ckv'], ss['mla_ckv'], sp['mla_krope'], ss['mla_krope'],
            sp['fox_k'], ss['fox_k'], sp['fox_v'], ss['fox_v'], sp['fox_logf'], ss['fox_logf'],
            sp['rwkv_S'], ss['rwkv_S'], sp['rwkv_shift'], ss['rwkv_shift'],
            sp['mlstm_C'], ss['mlstm_C'], sp['mlstm_n'], ss['mlstm_n'], sp['mlstm_m'], ss['mlstm_m'],
            sp['mlstm_conv'], ss['mlstm_conv'])
```
